```python
import numpy as np
import jax
import jax.numpy as jnp
from jax import lax

D_MODEL = 1024
BATCH = 2
SEQ = 16384
DEPTH = 2
DEC_BATCH = 32
DEC_SEQ = 64
PAST_LEN = 4096

CHUNK = 64
Q_BLOCK = 128
H_A = 4
DK_A = 32
DV_A = 64
W_A = H_A * DV_A
GATE_RANK = 16
GLA_TAU = 16.0
H_B = 4
DH_B = 64
W_B = H_B * DH_B
CONV_W = 4
FORGET_BIAS = 3.0
H_C = 8
KVH_C = 2
HD_C = 64
W_C = H_C * HD_C
H_IDX = 8
D_IDX = 32
TOPK_MAX = 256
MIX_W = W_A + W_B + W_C
N_EXPERTS = 16
N_GROUPS = 4
TOPK_EXPERTS = 2
GROUP_SCORE_TOPK = 2
D_FF_E = 256
ROPE_THETA = 10000.0
ALPHA = (2 * DEPTH) ** 0.25
BETA = (8 * DEPTH) ** -0.25
LN_EPS = 1e-5

kernel_name = 'hybrid_gla_mlstm_dsa_moe_stream_step'


def _proj_sizes():
    return [H_A * DK_A, H_A * DK_A, W_A, W_A, GATE_RANK,
            2 * W_B, W_B, W_B, H_B, H_B,
            W_C, KVH_C * HD_C, KVH_C * HD_C, H_IDX * D_IDX, D_IDX, H_IDX]


def layer_norm(x, g, b):
    xf = x.astype(jnp.float32)
    xc = xf - xf.mean(-1, keepdims=True)
    var = (xc * xc).mean(-1, keepdims=True)
    return (xc * lax.rsqrt(var + LN_EPS) * g + b).astype(x.dtype)


def head_norm(h, g, centered):
    if centered:
        h = h - h.mean(-1, keepdims=True)
    h = h * lax.rsqrt((h * h).mean(-1, keepdims=True) + LN_EPS)
    return h.reshape(h.shape[0], h.shape[1], -1) * g.astype(jnp.float32)


def rope(x, pos):
    half = x.shape[-1] // 2
    inv = jnp.power(ROPE_THETA, -jnp.arange(half, dtype=jnp.float32) / half)
    ang = pos.astype(jnp.float32)[:, None] * inv[None, :]
    cos = jnp.cos(ang)[None, :, None, :]
    sin = jnp.sin(ang)[None, :, None, :]
    xf = x.astype(jnp.float32)
    x1, x2 = xf[..., :half], xf[..., half:]
    return jnp.concatenate([x1 * cos - x2 * sin, x1 * sin + x2 * cos], -1).astype(x.dtype)


def _to_chunks(a, c):
    B, L = a.shape[:2]
    return jnp.moveaxis(a.reshape((B, L // c, c) + a.shape[2:]), 1, 0)


def _from_chunks(a):
    a = jnp.moveaxis(a, 0, 1)
    return a.reshape((a.shape[0], -1) + a.shape[3:])


def gla_scan(q, k, v, log_a, S0):
    L = q.shape[1]
    c = min(CHUNK, L)
    tri = jnp.tril(jnp.ones((c, c), dtype=bool))[None, :, :, None, None]

    def step(S, inp):
        qc, kc, vc, ac = inp
        bc = jnp.cumsum(ac, axis=1)
        inter = jnp.einsum('bthk,bhkv->bthv', qc * jnp.exp(bc), S)
        diff = bc[:, :, None] - bc[:, None, :]
        decay = jnp.where(tri, jnp.exp(jnp.where(tri, diff, 0.0)), 0.0)
        att = jnp.einsum('bthk,bshk,btshk->bths', qc, kc, decay)
        intra = jnp.einsum('bths,bshv->bthv', att, vc)
        bl = bc[:, -1]
        S_new = jnp.exp(bl)[..., None] * S + jnp.einsum('bshk,bshv->bhkv', kc * jnp.exp(bl[:, None] - bc), vc)
        return S_new, inter + intra

    S_fin, out = lax.scan(step, S0, (_to_chunks(q, c), _to_chunks(k, c), _to_chunks(v, c), _to_chunks(log_a, c)))
    return _from_chunks(out), S_fin


def mlstm_scan(q, k, v, logi, logf, C0, n0, m0):
    L = q.shape[1]
    c = min(CHUNK, L)
    tri = jnp.tril(jnp.ones((c, c), dtype=bool))[None, :, :, None]

    def step(carry, inp):
        C, n, m = carry
        qc, kc, vc, ic, fc = inp
        b = jnp.cumsum(fc, axis=1)
        g_inter = b + m[:, None]
        D = jnp.where(tri, b[:, :, None] - b[:, None, :] + ic[:, None, :], -jnp.inf)
        m_t = jnp.maximum(g_inter, D.max(axis=2))
        w_inter = jnp.exp(g_inter - m_t)
        qk = jnp.einsum('bthd,bshd->btsh', qc, kc) * jnp.exp(D - m_t[:, :, None])
        num = w_inter[..., None] * jnp.einsum('bthk,bhkv->bthv', qc, C) + jnp.einsum('btsh,bshv->bthv', qk, vc)
        den = w_inter * jnp.einsum('bthk,bhk->bth', qc, n) + qk.sum(axis=2)
        h = num / jnp.maximum(jnp.abs(den), jnp.exp(-m_t))[..., None]
        m_new = m_t[:, -1]
        w_C = jnp.exp(b[:, -1] + m - m_new)
        w_s = jnp.exp(b[:, -1:] - b + ic - m_new[:, None])
        C_new = w_C[..., None, None] * C + jnp.einsum('bsh,bshk,bshv->bhkv', w_s, kc, vc)
        n_new = w_C[..., None] * n + jnp.einsum('bsh,bshk->bhk', w_s, kc)
        return (C_new, n_new, m_new), h

    xs = (_to_chunks(q, c), _to_chunks(k, c), _to_chunks(v, c), _to_chunks(logi, c), _to_chunks(logf, c))
    (C_f, n_f, m_f), out = lax.scan(step, (C0, n0, m0), xs)
    return _from_chunks(out), C_f, n_f, m_f


def dsa_block(qb, qib, wib, limb, k_all, v_all, ki_all, topk):
    B, Q = qb.shape[:2]
    L = k_all.shape[1]
    rel = jax.nn.relu(jnp.einsum('bqhd,bsd->bqhs', qib.astype(jnp.float32), ki_all.astype(jnp.float32)) * D_IDX ** -0.5)
    score = jnp.einsum('bqh,bqhs->bqs', wib.astype(jnp.float32), rel)
    adm = jnp.arange(L)[None, :] < limb[:, None]
    score = jnp.where(adm[None], score, -jnp.inf)
    _, idx = lax.top_k(score, topk)
    valid = idx < limb[None, :, None]
    ksel = jax.vmap(lambda kk, ii: kk[ii])(k_all, idx)
    vsel = jax.vmap(lambda vv, ii: vv[ii])(v_all, idx)
    qg = qb.reshape(B, Q, KVH_C, H_C // KVH_C, HD_C)
    logits = jnp.einsum('bqkgd,bqskd->bqkgs', qg, ksel).astype(jnp.float32) * HD_C ** -0.5
    logits = jnp.where(valid[:, :, None, None, :], logits, -jnp.inf)
    prob = jax.nn.softmax(logits, axis=-1).astype(vsel.dtype)
    o = jnp.einsum('bqkgs,bqskd->bqkgd', prob, vsel)
    return o.reshape(B, Q, W_C)


def token_mixer(u, pos0, p, past):
    f32 = jnp.float32
    B, T, _ = u.shape
    pos = pos0 + jnp.arange(T)
    proj = jnp.einsum('btd,dp->btp', u, p['w_in']) + p['b_in']
    (qa, ka, va, ga, gr, qk_pre, vb, ob, ib, fb, qc, kc, vc, qi, ki, wi) = jnp.split(
        proj, np.cumsum(_proj_sizes())[:-1].tolist(), axis=-1)

    qa = qa.astype(f32).reshape(B, T, H_A, DK_A) * DK_A ** -0.5
    ka = ka.astype(f32).reshape(B, T, H_A, DK_A)
    va = va.astype(f32).reshape(B, T, H_A, DV_A)
    log_a = jax.nn.log_sigmoid((jnp.einsum('btr,rk->btk', gr, p['gla_w_up']) + p['gla_b_up']).astype(f32))
    log_a = log_a.reshape(B, T, H_A, DK_A) / GLA_TAU
    S0 = jnp.zeros((B, H_A, DK_A, DV_A), f32) if past is None else past['gla'].astype(f32)
    oa, S_new = gla_scan(qa, ka, va, log_a, S0)
    oa = head_norm(oa, p['gla_norm_g'], False) * jax.nn.silu(ga.astype(f32))

    buf = jnp.zeros((B, CONV_W - 1, 2 * W_B), qk_pre.dtype) if past is None else past['conv'].astype(qk_pre.dtype)
    xpad = jnp.concatenate([buf, qk_pre], axis=1)
    conv = sum(xpad[:, j:j + T] * p['mlstm_conv_w'][j] for j in range(CONV_W)) + p['mlstm_conv_b']
    conv_buf = xpad[:, T:]
    qb, kb = jnp.split(jax.nn.silu(conv.astype(f32)), 2, axis=-1)
    qb = qb.reshape(B, T, H_B, DH_B)
    kb = kb.reshape(B, T, H_B, DH_B) * DH_B ** -0.5
    vb = vb.astype(f32).reshape(B, T, H_B, DH_B)
    if past is None:
        C0 = jnp.zeros((B, H_B, DH_B, DH_B), f32)
        n0 = jnp.zeros((B, H_B, DH_B), f32)
        m0 = jnp.zeros((B, H_B), f32)
    else:
        C0, n0, m0 = past['C'].astype(f32), past['n'].astype(f32), past['m'].astype(f32)
    hb, C_new, n_new, m_new = mlstm_scan(qb, kb, vb, ib.astype(f32), jax.nn.log_sigmoid(fb.astype(f32)), C0, n0, m0)
    hb = head_norm(hb, p['mlstm_norm_g'], True) * jax.nn.sigmoid(ob.astype(f32))

    qc = rope(qc.reshape(B, T, H_C, HD_C), pos)
    kc = rope(kc.reshape(B, T, KVH_C, HD_C), pos)
    vc = vc.reshape(B, T, KVH_C, HD_C)
    qi = rope(qi.reshape(B, T, H_IDX, D_IDX), pos)
    ki = rope(ki[:, :, None, :], pos)[:, :, 0]
    wi = wi * H_IDX ** -0.5
    if past is None:
        k_all, v_all, ki_all = kc, vc, ki
    else:
        k_all = jnp.concatenate([past['k'].astype(kc.dtype), kc], axis=1)
        v_all = jnp.concatenate([past['v'].astype(vc.dtype), vc], axis=1)
        ki_all = jnp.concatenate([past['kidx'].astype(ki.dtype), ki], axis=1)
    L = k_all.shape[1]
    topk = min(TOPK_MAX, L // 4)
    limit = jnp.minimum((pos // CHUNK + 1) * CHUNK, L)
    qbs = min(Q_BLOCK, T)
    nblk = T // qbs
    blk = lambda a: jnp.moveaxis(a.reshape((B, nblk, qbs) + a.shape[2:]), 1, 0)
    oc = lax.map(lambda a: dsa_block(a[0], a[1], a[2], a[3], k_all, v_all, ki_all, topk),
                 (blk(qc), blk(qi), blk(wi), limit.reshape(nblk, qbs)))
    oc = jnp.moveaxis(oc, 0, 1).reshape(B, T, W_C)

    mixed = jnp.concatenate([oa, hb, oc.astype(f32)], axis=-1).astype(u.dtype)
    out = jnp.einsum('btm,md->btd', mixed, p['w_out'])
    return out, (kc, vc, ki, S_new, C_new, n_new, m_new, conv_buf)


def route(u, w_router, b_router):
    f32 = jnp.float32
    B, T, _ = u.shape
    s = jax.nn.sigmoid(jnp.einsum('btd,de->bte', u.astype(f32), w_router.astype(f32)))
    sb = s + b_router.astype(f32)
    gsc = lax.top_k(sb.reshape(B, T, N_GROUPS, N_EXPERTS // N_GROUPS), GROUP_SCORE_TOPK)[0].sum(-1)
    g_sel = jnp.argmax(gsc, axis=-1)
    in_group = (jnp.arange(N_EXPERTS) // (N_EXPERTS // N_GROUPS))[None, None, :] == g_sel[..., None]
    _, eidx = lax.top_k(jnp.where(in_group, sb, -jnp.inf), TOPK_EXPERTS)
    w = jnp.take_along_axis(s, eidx, axis=-1)
    w = w / w.sum(-1, keepdims=True)
    return jnp.einsum('btk,btke->bte', w, jax.nn.one_hot(eidx, N_EXPERTS, dtype=f32))


def moe(u, p):
    gates = route(u, p['w_router'], p['b_router']).astype(u.dtype)
    out = jnp.zeros_like(u)
    for e in range(N_EXPERTS):
        h = jax.nn.silu(u @ p['w_gate'][e]) * (u @ p['w_up'][e])
        out = out + gates[..., e:e + 1] * (h @ p['w_down'][e])
    return out


def trunk_layer(x, c, pos0, p, past):
    mod = jnp.einsum('bd,de->be', jax.nn.silu(c), p['w_ada']) + p['b_ada']
    sh1, sc1, g1, sh2, sc2, g2 = jnp.split(mod[:, None, :], 6, axis=-1)
    mix, new_state = token_mixer(x * (1 + sc1) + sh1, pos0, p, past)
    x = layer_norm(ALPHA * x + (1 + g1) * mix, p['ln1_g'], p['ln1_b'])
    ffn = moe(x * (1 + sc2) + sh2, p)
    x = layer_norm(ALPHA * x + (1 + g2) * ffn, p['ln2_g'], p['ln2_b'])
    return x, new_state


def setup_inputs(seed: int = 0) -> dict:
    key = jax.random.key(seed)
    keys = iter(jax.random.split(key, 48))

    def nrm(shape, scale):
        return scale * jax.random.normal(next(keys), shape, jnp.float32)

    sizes = _proj_sizes()
    P = sum(sizes)
    f_off = sum(sizes[:9])
    D = D_MODEL
    return {
        'x_prompt': nrm((BATCH, SEQ, D), 1.0),
        'x_sample': nrm((DEC_BATCH, DEC_SEQ, D), 1.0),
        'c_prompt': nrm((BATCH, D), 1.0),
        'c_sample': nrm((DEC_BATCH, D), 1.0),
        'cache_k': nrm((DEPTH, DEC_BATCH, PAST_LEN, KVH_C, HD_C), 1.0),
        'cache_v': nrm((DEPTH, DEC_BATCH, PAST_LEN, KVH_C, HD_C), 1.0),
        'cache_kidx': nrm((DEPTH, DEC_BATCH, PAST_LEN, D_IDX), 1.0),
        'state_gla': nrm((DEPTH, DEC_BATCH, H_A, DK_A, DV_A), 0.3),
        'state_mlstm_C': nrm((DEPTH, DEC_BATCH, H_B, DH_B, DH_B), 0.3),
        'state_mlstm_n': nrm((DEPTH, DEC_BATCH, H_B, DH_B), 0.3),
        'state_mlstm_m': nrm((DEPTH, DEC_BATCH, H_B), 0.5),
        'state_mlstm_conv': nrm((DEPTH, DEC_BATCH, CONV_W - 1, 2 * W_B), 1.0),
        'ln_in_g': 1.0 + nrm((D,), 0.02),
        'ln_in_b': nrm((D,), 0.02),
        'w_ada': nrm((DEPTH, D, 6 * D), 0.2 * D ** -0.5),
        'b_ada': nrm((DEPTH, 6 * D), 0.02),
        'w_in': nrm((DEPTH, D, P), D ** -0.5),
        'b_in': nrm((DEPTH, P), 0.02).at[:, f_off:f_off + H_B].add(FORGET_BIAS),
        'gla_w_up': nrm((DEPTH, GATE_RANK, H_A * DK_A), GATE_RANK ** -0.5),
        'gla_b_up': nrm((DEPTH, H_A * DK_A), 0.1),
        'gla_norm_g': 1.0 + nrm((DEPTH, W_A), 0.02),
        'mlstm_conv_w': nrm((DEPTH, CONV_W, 2 * W_B), CONV_W ** -0.5),
        'mlstm_conv_b': nrm((DEPTH, 2 * W_B), 0.02),
        'mlstm_norm_g': 1.0 + nrm((DEPTH, W_B), 0.02),
        'w_out': nrm((DEPTH, MIX_W, D), BETA * MIX_W ** -0.5),
        'ln1_g': 1.0 + nrm((DEPTH, D), 0.02),
        'ln1_b': nrm((DEPTH, D), 0.02),
        'w_router': nrm((D, N_EXPERTS), D ** -0.5),
        'b_router': nrm((N_EXPERTS,), 0.01),
        'w_gate': nrm((DEPTH, N_EXPERTS, D, D_FF_E), D ** -0.5),
        'w_up': nrm((DEPTH, N_EXPERTS, D, D_FF_E), D ** -0.5),
        'w_down': nrm((DEPTH, N_EXPERTS, D_FF_E, D), BETA * D_FF_E ** -0.5),
        'ln2_g': 1.0 + nrm((DEPTH, D), 0.02),
        'ln2_b': nrm((DEPTH, D), 0.02),
    }


def reference(x_prompt, x_sample, c_prompt, c_sample,
              cache_k, cache_v, cache_kidx, state_gla, state_mlstm_C, state_mlstm_n, state_mlstm_m, state_mlstm_conv,
              ln_in_g, ln_in_b, w_ada, b_ada, w_in, b_in, gla_w_up, gla_b_up, gla_norm_g,
              mlstm_conv_w, mlstm_conv_b, mlstm_norm_g, w_out, ln1_g, ln1_b, w_router, b_router,
              w_gate, w_up, w_down, ln2_g, ln2_b):
    def params(l):
        return dict(w_ada=w_ada[l], b_ada=b_ada[l], w_in=w_in[l], b_in=b_in[l],
                    gla_w_up=gla_w_up[l], gla_b_up=gla_b_up[l], gla_norm_g=gla_norm_g[l],
                    mlstm_conv_w=mlstm_conv_w[l], mlstm_conv_b=mlstm_conv_b[l], mlstm_norm_g=mlstm_norm_g[l],
                    w_out=w_out[l], ln1_g=ln1_g[l], ln1_b=ln1_b[l],
                    w_router=w_router, b_router=b_router,
                    w_gate=w_gate[l], w_up=w_up[l], w_down=w_down[l], ln2_g=ln2_g[l], ln2_b=ln2_b[l])

    past_len = cache_k.shape[2]
    xp = layer_norm(x_prompt, ln_in_g, ln_in_b)
    xs = layer_norm(x_sample, ln_in_g, ln_in_b)
    p_states, s_states = [], []
    for l in range(DEPTH):
        p = params(l)
        xp, st_p = trunk_layer(xp, c_prompt, 0, p, None)
        p_states.append(st_p)
        past = dict(k=cache_k[l], v=cache_v[l], kidx=cache_kidx[l], gla=state_gla[l],
                    C=state_mlstm_C[l], n=state_mlstm_n[l], m=state_mlstm_m[l], conv=state_mlstm_conv[l])
        xs, st_s = trunk_layer(xs, c_sample, past_len, p, past)
        s_states.append(st_s)

    def stk(states, i):
        return jnp.stack([st[i] for st in states], axis=0)

    return (xp, xs,
            stk(p_states, 0), stk(p_states, 1), stk(p_states, 2), stk(p_states, 3),
            stk(p_states, 4), stk(p_states, 5), stk(p_states, 6), stk(p_states, 7),
            stk(s_states, 0), stk(s_states, 1), stk(s_states, 2), stk(s_states, 3),
            stk(s_states, 4), stk(s_states, 5), stk(s_states, 6), stk(s_states, 7))
```

```python
import functools

import numpy as np
import jax
import jax.numpy as jnp
from jax import lax
from jax.experimental import pallas as pl
from jax.experimental.pallas import tpu as pltpu

F32 = jnp.float32
BF16 = jnp.bfloat16
I32 = jnp.int32
HIGHEST = lax.Precision.HIGHEST

CHUNK = 64
H_A, DK_A, DV_A = 4, 32, 64
W_A = H_A * DV_A
GATE_RANK = 16
GLA_TAU = 16.0
H_B, DH_B = 4, 64
W_B = H_B * DH_B
CONV_W = 4
H_C, KVH_C, HD_C = 8, 2, 64
W_C = H_C * HD_C
H_IDX, D_IDX = 8, 32
TOPK_MAX = 256
N_EXPERTS, N_GROUPS = 16, 4
D_FF_E = 256
ROPE_THETA = 10000.0
LN_EPS = 1e-5
DEPTH = 2
ALPHA = (2 * DEPTH) ** 0.25

V7X_LANES = 128
V7X_VMEM_BYTES = 64 * 1024 * 1024
VMEM_LIMIT = (V7X_VMEM_BYTES * 3) // 4

PA_W = 896
PB_W = 1152
PC_W = 1152
P_TOTAL = PA_W + PB_W + PC_W

KEY_TILE = 512
INT_MIN = -2 ** 31


def _tile(n, pref):
    t = min(n, pref)
    while n % t:
        t //= 2
    return t


def _params(sem):
    return pltpu.CompilerParams(dimension_semantics=sem, vmem_limit_bytes=VMEM_LIMIT)


def _dot(a, b):
    return jnp.dot(a.astype(BF16), b.astype(BF16), preferred_element_type=F32)


def _dot_nt(a, b):
    return lax.dot_general(a.astype(BF16), b.astype(BF16), (((1,), (1,)), ((), ())), preferred_element_type=F32)


def _dot_tn(a, b):
    return lax.dot_general(a.astype(BF16), b.astype(BF16), (((0,), (0,)), ((), ())), preferred_element_type=F32)


def _dot_exact(a, b):
    return jnp.dot(a, b, preferred_element_type=F32, precision=HIGHEST)


def _log_sigmoid(x):
    return jnp.minimum(x, 0.0) - jnp.log1p(jnp.exp(-jnp.abs(x)))


def _sigmoid(x):
    return 1.0 / (1.0 + jnp.exp(-x))


def _silu(x):
    return x * _sigmoid(x)


def _layer_norm(x, g, b):
    xc = x - jnp.mean(x, axis=-1, keepdims=True)
    var = jnp.mean(xc * xc, axis=-1, keepdims=True)
    return xc * lax.rsqrt(var + LN_EPS) * g + b


def _ada_kernel(c_ref, w_ref, b_ref, o_ref):
    o_ref[0] = _dot(_silu(c_ref[...]), w_ref[0]) + b_ref[0]


def _ada(c_all, w_ada, b_ada):
    depth, d, n6 = w_ada.shape
    nb = c_all.shape[0]
    tn = 1024
    return pl.pallas_call(
        _ada_kernel,
        grid=(depth, n6 // tn),
        in_specs=[
            pl.BlockSpec((nb, d), lambda l, j: (0, 0)),
            pl.BlockSpec((1, d, tn), lambda l, j: (l, 0, j)),
            pl.BlockSpec((1, 1, tn), lambda l, j: (l, 0, j)),
        ],
        out_specs=pl.BlockSpec((1, nb, tn), lambda l, j: (l, 0, j)),
        out_shape=jax.ShapeDtypeStruct((depth, nb, n6), F32),
        compiler_params=_params(("arbitrary", "arbitrary")),
        name="ada_mod",
    )(c_all, w_ada, b_ada.reshape(depth, 1, n6))


def _rot_half(x, half):
    lane = lax.broadcasted_iota(I32, x.shape, 1)
    first = (lane % (2 * half)) < half
    return jnp.where(first, pltpu.roll(x, V7X_LANES - half, 1), pltpu.roll(x, half, 1))


def _inproj_kernel(x_ref, mod_ref, g_ref, b_ref, w_ref, bias_ref, t64_ref, t32_ref, tm_ref,
                   *out_refs, apply_ln):
    if apply_ln:
        xln_ref, pa_ref, pb_ref, pc_ref = out_refs
    else:
        pa_ref, pb_ref, pc_ref = out_refs
    x = x_ref[0]
    if apply_ln:
        x = _layer_norm(x, g_ref[...], b_ref[...])
        xln_ref[0] = x
    u = (x * (1.0 + mod_ref[0, 1:2, :]) + mod_ref[0, 0:1, :]).astype(BF16)
    pa_ref[0] = jnp.dot(u, w_ref[:, 0:PA_W], preferred_element_type=F32) + bias_ref[:, 0:PA_W]
    pb_ref[0] = (jnp.dot(u, w_ref[:, PA_W:PA_W + PB_W], preferred_element_type=F32)
                 + bias_ref[:, PA_W:PA_W + PB_W])
    pc = (jnp.dot(u, w_ref[:, PA_W + PB_W:P_TOTAL], preferred_element_type=F32)
          + bias_ref[:, PA_W + PB_W:P_TOTAL])
    c64, s64 = t64_ref[:, 0:128], t64_ref[:, 128:256]
    c32, s32 = t32_ref[:, 0:128], t32_ref[:, 128:256]
    cm, sm = tm_ref[:, 0:128], tm_ref[:, 128:256]
    for s in range(5):
        xs = pc[:, s * 128:(s + 1) * 128]
        pc_ref[0, :, s * 128:(s + 1) * 128] = xs * c64 + _rot_half(xs, HD_C // 2) * s64
    pc_ref[0, :, 640:768] = pc[:, 640:768]
    for s in range(6, 8):
        xs = pc[:, s * 128:(s + 1) * 128]
        pc_ref[0, :, s * 128:(s + 1) * 128] = xs * c32 + _rot_half(xs, D_IDX // 2) * s32
    xs = pc[:, 1024:1152]
    pc_ref[0, :, 1024:1152] = xs * cm + _rot_half(xs, D_IDX // 2) * sm


def _inproj(x, mod, ln_g, ln_b, w_p, bias_p, tabs, apply_ln):
    b, t, d = x.shape
    tm = _tile(t, 512)
    t64, t32, tmisc = tabs
    row = lambda bi, i: (bi, i, 0)
    out_shape = [jax.ShapeDtypeStruct((b, t, PA_W), F32), jax.ShapeDtypeStruct((b, t, PB_W), F32),
                 jax.ShapeDtypeStruct((b, t, PC_W), F32)]
    out_specs = [pl.BlockSpec((1, tm, PA_W), row), pl.BlockSpec((1, tm, PB_W), row),
                 pl.BlockSpec((1, tm, PC_W), row)]
    if apply_ln:
        out_shape = [jax.ShapeDtypeStruct((b, t, d), F32)] + out_shape
        out_specs = [pl.BlockSpec((1, tm, d), row)] + out_specs
    return pl.pallas_call(
        functools.partial(_inproj_kernel, apply_ln=apply_ln),
        grid=(b, t // tm),
        in_specs=[
            pl.BlockSpec((1, tm, d), row),
            pl.BlockSpec((1, 6, d), lambda bi, i: (bi, 0, 0)),
            pl.BlockSpec((1, d), lambda bi, i: (0, 0)),
            pl.BlockSpec((1, d), lambda bi, i: (0, 0)),
            pl.BlockSpec((d, P_TOTAL), lambda bi, i: (0, 0)),
            pl.BlockSpec((1, P_TOTAL), lambda bi, i: (0, 0)),
            pl.BlockSpec((tm, 256), lambda bi, i: (i, 0)),
            pl.BlockSpec((tm, 256), lambda bi, i: (i, 0)),
            pl.BlockSpec((tm, 256), lambda bi, i: (i, 0)),
        ],
        out_specs=out_specs,
        out_shape=out_shape,
        compiler_params=_params(("arbitrary", "arbitrary")),
        name="in_proj",
    )(x, mod, ln_g, ln_b, w_p, bias_p, t64, t32, tmisc)


def _seg_sum(x, ones_bd):
    hi = x.astype(BF16)
    lo = (x - hi.astype(F32)).astype(BF16)
    return (jnp.dot(hi, ones_bd, preferred_element_type=F32)
            + jnp.dot(lo, ones_bd, preferred_element_type=F32))


def _gla_kernel(pa_ref, s0_ref, wup_ref, bup_ref, gn_ref, tril_ref, ones_ref, oa_ref, sfin_ref, st_ref, *, nc):
    i = pl.program_id(1)

    @pl.when(i == 0)
    def _():
        st_ref[...] = jnp.zeros_like(st_ref)
        for h in range(H_A):
            st_ref[h * DV_A:(h + 1) * DV_A, h * DK_A:(h + 1) * DK_A] = s0_ref[0, h]

    tril = tril_ref[...]
    ones_bd = ones_ref[...]
    r_sk = lax.broadcasted_iota(I32, (H_A * CHUNK, H_A * DK_A), 0) // CHUNK
    c_sk = lax.broadcasted_iota(I32, (H_A * CHUNK, H_A * DK_A), 1) // DK_A
    mask_sk = r_sk == c_sk
    r_sv = lax.broadcasted_iota(I32, (H_A * CHUNK, W_A), 0) // CHUNK
    c_sv = lax.broadcasted_iota(I32, (H_A * CHUNK, W_A), 1) // DV_A
    mask_sv = r_sv == c_sv
    r_st = lax.broadcasted_iota(I32, (W_A, H_A * DK_A), 0) // DV_A
    c_st = lax.broadcasted_iota(I32, (W_A, H_A * DK_A), 1) // DK_A
    mask_st = r_st == c_st
    t_i = lax.broadcasted_iota(I32, (CHUNK, H_A * CHUNK), 0)
    s_i = lax.broadcasted_iota(I32, (CHUNK, H_A * CHUNK), 1) % CHUNK
    causal = t_i >= s_i

    st = st_ref[...]
    for c in range(nc):
        r = slice(c * CHUNK, (c + 1) * CHUNK)
        q = pa_ref[0, r, 0:128] * (DK_A ** -0.5)
        k = pa_ref[0, r, 128:256]
        v = pa_ref[0, r, 256:512]
        g = pa_ref[0, r, 512:768]
        misc = pa_ref[0, r, 768:896]
        pre = _dot(misc, wup_ref[...]) + bup_ref[...]
        log_a = _log_sigmoid(pre) * (1.0 / GLA_TAU)
        bc = _dot_exact(tril, log_a)
        bm = bc[CHUNK // 2 - 1:CHUNK // 2, :]
        bl = bc[CHUNK - 1:CHUNK, :]
        inter = _dot_nt(q * jnp.exp(bc), st)
        qs = q * jnp.exp(bc - bm)
        ks = k * jnp.exp(bm - bc)
        ks_bd = jnp.where(mask_sk, jnp.concatenate([ks] * H_A, axis=0), 0.0)
        att = jnp.where(causal, _dot_nt(qs, ks_bd), 0.0)
        v_bd = jnp.where(mask_sv, jnp.concatenate([v] * H_A, axis=0), 0.0)
        o = inter + _dot(att, v_bd)
        ms = _seg_sum(o * o, ones_bd) * (1.0 / DV_A)
        oa_ref[0, r, :] = o * lax.rsqrt(ms + LN_EPS) * gn_ref[...] * _silu(g)
        upd = _dot_tn(v, k * jnp.exp(bl - bc))
        st = jnp.exp(bl) * st + jnp.where(mask_st, upd, 0.0)
    st_ref[...] = st
    for h in range(H_A):
        sfin_ref[0, h] = st[h * DV_A:(h + 1) * DV_A, h * DK_A:(h + 1) * DK_A]


def _gla(pa, s0_t, wup_p, bup, gnorm, consts):
    b, t, _ = pa.shape
    tb = _tile(t, 512)
    tril, ones_bd = consts
    full = lambda shape: pl.BlockSpec(shape, lambda bi, i: (0,) * len(shape))
    return pl.pallas_call(
        functools.partial(_gla_kernel, nc=tb // CHUNK),
        grid=(b, t // tb),
        in_specs=[
            pl.BlockSpec((1, tb, PA_W), lambda bi, i: (bi, i, 0)),
            pl.BlockSpec((1, H_A, DV_A, DK_A), lambda bi, i: (bi, 0, 0, 0)),
            full((128, 128)), full((1, 128)), full((1, W_A)), full((CHUNK, CHUNK)), full((W_A, W_A)),
        ],
        out_specs=[
            pl.BlockSpec((1, tb, W_A), lambda bi, i: (bi, i, 0)),
            pl.BlockSpec((1, H_A, DV_A, DK_A), lambda bi, i: (bi, 0, 0, 0)),
        ],
        out_shape=[jax.ShapeDtypeStruct((b, t, W_A), F32),
                   jax.ShapeDtypeStruct((b, H_A, DV_A, DK_A), F32)],
        scratch_shapes=[pltpu.VMEM((W_A, H_A * DK_A), F32)],
        compiler_params=_params(("arbitrary", "arbitrary")),
        name="gla_scan",
    )(pa, s0_t, wup_p, bup, gnorm, tril, ones_bd)


MST_W = W_B + V7X_LANES


def _expand_heads(cols, lane_head):
    out = jnp.zeros(lane_head.shape, F32)
    for h, col in enumerate(cols):
        out = jnp.where(lane_head == h, col, out)
    return out


def _mlstm_kernel(pb_ref, conv0_ref, cw_ref, cb_ref, c0_ref, m0_ref, gn_ref, tril_ref, ones_ref,
                  hb_ref, convo_ref, cfin_ref, mfin_ref, xs_ref, cst_ref, m_ref, *, nc):
    i = pl.program_id(1)
    tb = nc * CHUNK

    @pl.when(i == 0)
    def _():
        xs_ref[0:8, :] = conv0_ref[0]
        cst_ref[...] = c0_ref[0]
        m_ref[...] = m0_ref[0]

    xs_ref[8:8 + tb, :] = pb_ref[0, :, 0:512]
    tril = tril_ref[...]
    ones_bd = ones_ref[...]

    lane_h = lax.broadcasted_iota(I32, (CHUNK, W_B), 1) // DH_B
    t_i = lax.broadcasted_iota(I32, (CHUNK, W_B), 0)
    s_i = lax.broadcasted_iota(I32, (CHUNK, W_B), 1) % CHUNK
    causal = t_i >= s_i
    diag = t_i == s_i
    e_r = lax.broadcasted_iota(I32, (V7X_LANES, W_B), 0)
    e_c = lax.broadcasted_iota(I32, (V7X_LANES, W_B), 1) // DH_B
    e_f = (e_r == e_c + H_B).astype(F32)
    e_i = (e_r == e_c).astype(F32)
    bd_r = lax.broadcasted_iota(I32, (W_B, W_B), 0) // DH_B
    bd_c = lax.broadcasted_iota(I32, (W_B, W_B), 1) // DH_B
    mask_bd = bd_r == bd_c
    st_lane = lax.broadcasted_iota(I32, (CHUNK, MST_W), 1)
    st_lane_h = jnp.where(st_lane < W_B, st_lane // DH_B, jnp.where(st_lane < W_B + H_B, st_lane - W_B, -1))
    row_h = lax.broadcasted_iota(I32, (W_B, MST_W), 0) // DH_B
    col = lax.broadcasted_iota(I32, (W_B, MST_W), 1)
    col_h = jnp.where(col < W_B, col // DH_B, jnp.where(col < W_B + H_B, col - W_B, -1))
    mask_state = row_h == col_h
    ones_lane = (lax.broadcasted_iota(I32, (CHUNK, V7X_LANES), 1) < H_B).astype(F32)
    ones_sq = jnp.ones((CHUNK, CHUNK), F32)

    cst = cst_ref[...]
    m_prev = [m_ref[h:h + 1, 0:1] for h in range(H_B)]
    for c in range(nc):
        base = 8 + c * CHUNK
        r = slice(c * CHUNK, (c + 1) * CHUNK)
        conv = cb_ref[...]
        for j in range(CONV_W):
            conv = conv + xs_ref[base - (CONV_W - 1) + j:base - (CONV_W - 1) + j + CHUNK, :] * cw_ref[j:j + 1, :]
        act = _silu(conv)
        q = act[:, 0:W_B]
        k = act[:, W_B:2 * W_B] * (DH_B ** -0.5)
        v = pb_ref[0, r, 512:768]
        og = pb_ref[0, r, 768:1024]
        gates = pb_ref[0, r, 1024:1152]
        bcum = _dot_exact(tril, _log_sigmoid(gates))
        b_exp = _dot_exact(bcum, e_f)
        i_exp = _dot_exact(gates, e_i)
        row_term = _dot_exact(ones_sq, jnp.where(diag, i_exp - b_exp, 0.0))
        dmat = jnp.where(causal, b_exp + row_term, -jnp.inf)
        bcol = [bcum[:, H_B + h:H_B + h + 1] for h in range(H_B)]
        icol = [gates[:, h:h + 1] for h in range(H_B)]
        ginter = [bcol[h] + m_prev[h] for h in range(H_B)]
        mt = [jnp.maximum(ginter[h], jnp.max(jnp.where(lane_h == h, dmat, -jnp.inf), axis=1, keepdims=True))
              for h in range(H_B)]
        mt_exp = _expand_heads(mt, lane_h)
        k_bd = jnp.where(mask_bd, jnp.concatenate([k] * H_B, axis=0), 0.0)
        qk = _dot_nt(q, k_bd) * jnp.exp(dmat - mt_exp)
        v_aug = jnp.concatenate([v, ones_lane], axis=1)
        v_bd = jnp.where(mask_state, jnp.concatenate([v_aug] * H_B, axis=0), 0.0)
        w_inter = _expand_heads([jnp.exp(ginter[h] - mt[h]) for h in range(H_B)], st_lane_h)
        tot = w_inter * _dot(q, cst) + _dot(qk, v_bd)
        num = tot[:, 0:W_B]
        den = _expand_heads([tot[:, W_B + h:W_B + h + 1] for h in range(H_B)], lane_h)
        hh = num / jnp.maximum(jnp.abs(den), jnp.exp(-mt_exp))
        mu = _seg_sum(hh, ones_bd) * (1.0 / DH_B)
        hc = hh - mu
        var = _seg_sum(hc * hc, ones_bd) * (1.0 / DH_B)
        hb_ref[0, r, :] = hc * lax.rsqrt(var + LN_EPS) * gn_ref[...] * _sigmoid(og)
        m_new = [mt[h][CHUNK - 1:CHUNK, :] for h in range(H_B)]
        b_last = [bcol[h][CHUNK - 1:CHUNK, :] for h in range(H_B)]
        w_c = [jnp.exp(b_last[h] + m_prev[h] - m_new[h]) for h in range(H_B)]
        w_s = _expand_heads([jnp.exp(b_last[h] - bcol[h] + icol[h] - m_new[h]) for h in range(H_B)], lane_h)
        upd = _dot_tn(k * w_s, v_aug)
        w_c_rows = jnp.zeros((W_B, MST_W), F32)
        for h in range(H_B):
            w_c_rows = jnp.where(row_h == h, w_c[h], w_c_rows)
        cst = w_c_rows * cst + jnp.where(mask_state, upd, 0.0)
        m_prev = m_new

    cst_ref[...] = cst
    for h in range(H_B):
        m_ref[h:h + 1, :] = jnp.broadcast_to(m_prev[h], (1, V7X_LANES))
    tail = xs_ref[tb:tb + 8, :]
    xs_ref[0:8, :] = tail
    convo_ref[0] = tail
    cfin_ref[0] = cst
    mfin_ref[0] = m_ref[...]


def _mlstm(pb, conv0_p, cw_p, cb, c0_st, m0_p, gnorm, consts):
    b, t, _ = pb.shape
    tb = _tile(t, 512)
    tril, ones_bd = consts
    full = lambda shape: pl.BlockSpec(shape, lambda bi, i: (0,) * len(shape))
    per_b = lambda shape: pl.BlockSpec((1,) + shape, lambda bi, i: (bi,) + (0,) * len(shape))
    return pl.pallas_call(
        functools.partial(_mlstm_kernel, nc=tb // CHUNK),
        grid=(b, t // tb),
        in_specs=[
            pl.BlockSpec((1, tb, PB_W), lambda bi, i: (bi, i, 0)),
            per_b((8, 2 * W_B)), full((8, 2 * W_B)), full((1, 2 * W_B)),
            per_b((W_B, MST_W)), per_b((8, V7X_LANES)), full((1, W_B)), full((CHUNK, CHUNK)), full((W_B, W_B)),
        ],
        out_specs=[
            pl.BlockSpec((1, tb, W_B), lambda bi, i: (bi, i, 0)),
            per_b((8, 2 * W_B)), per_b((W_B, MST_W)), per_b((8, V7X_LANES)),
        ],
        out_shape=[jax.ShapeDtypeStruct((b, t, W_B), F32),
                   jax.ShapeDtypeStruct((b, 8, 2 * W_B), F32),
                   jax.ShapeDtypeStruct((b, W_B, MST_W), F32),
                   jax.ShapeDtypeStruct((b, 8, V7X_LANES), F32)],
        scratch_shapes=[pltpu.VMEM((tb + 8, 2 * W_B), F32), pltpu.VMEM((W_B, MST_W), F32),
                        pltpu.VMEM((8, V7X_LANES), F32)],
        compiler_params=_params(("arbitrary", "arbitrary")),
        name="mlstm_scan",
    )(pb, conv0_p, cw_p, cb, c0_st, m0_p, gnorm, tril, ones_bd)


def _dsa_kernel(q_ref, qi_ref, wi_ref, kt_ref, v_ref, kit_ref, o_ref, key_ref,
                *, tq, topk, pos0, l_true, n_tiles_max):
    qb = pl.program_id(1)
    lt = KEY_TILE
    t_row = lax.broadcasted_iota(I32, (tq, 1), 0) + qb * tq + pos0
    limit = jnp.minimum((t_row // CHUNK + 1) * CHUNK, l_true)
    last_limit = jnp.minimum(((qb * tq + tq - 1 + pos0) // CHUNK + 1) * CHUNK, l_true)
    nt = (last_limit + lt - 1) // lt
    lane = lax.broadcasted_iota(I32, (tq, lt), 1)

    qi = qi_ref[0].reshape(H_IDX * tq, D_IDX).astype(BF16)
    wcol = wi_ref[0].reshape(H_IDX * tq, 1) * (H_IDX ** -0.5 * D_IDX ** -0.5)

    def score_tile(j, carry):
        s = jnp.dot(qi, kit_ref[0, j], preferred_element_type=F32)
        s = jnp.maximum(s, 0.0) * wcol
        sc = s[0:tq]
        for h in range(1, H_IDX):
            sc = sc + s[h * tq:(h + 1) * tq]
        sc = jnp.where(sc == 0.0, 0.0, sc)
        bits = pltpu.bitcast(sc, I32)
        key = bits ^ ((bits >> 31) & jnp.int32(0x7FFFFFFF))
        key_ref[j] = jnp.where(lane + j * lt < limit, key, jnp.int32(INT_MIN))
        return carry

    lax.fori_loop(0, nt, score_tile, 0)

    def count(pred):
        def body(j, acc):
            kt = key_ref[j]
            for s in range(lt // V7X_LANES):
                sl = slice(s * V7X_LANES, (s + 1) * V7X_LANES)
                acc = acc + jnp.where(pred(kt[:, sl], j * lt + s * V7X_LANES), 1, 0)
            return acc
        acc = lax.fori_loop(0, nt, body, jnp.zeros((tq, V7X_LANES), I32))
        return jnp.sum(acc, axis=1, keepdims=True)

    def count_ge(cand):
        cb = jnp.broadcast_to(cand, (tq, V7X_LANES))
        return count(lambda kk, off: kk >= cb)

    min_col = jnp.full((tq, 1), INT_MIN, I32)
    c0 = count_ge(jnp.zeros((tq, 1), I32))
    ok0 = c0 >= topk
    ans = jnp.where(ok0, 0, min_col)

    def bis_body(it, ans):
        cand = ans + (jnp.int32(1) << (30 - it))
        cnt = count_ge(cand)
        return jnp.where(cnt >= topk, cand, ans)

    thr = lax.fori_loop(0, 31, bis_body, ans)
    thr_b = jnp.broadcast_to(thr, (tq, V7X_LANES))
    c_gt = count(lambda kk, off: kk > thr_b)
    c_ge = count(lambda kk, off: kk >= thr_b)
    need = topk - c_gt
    tie_rows = (c_ge > topk) & (thr > INT_MIN)
    lane128 = lax.broadcasted_iota(I32, (tq, V7X_LANES), 1)

    def tie_cut():
        def body(it, jcut):
            cand = jcut + (jnp.int32(1) << (14 - it))
            cb = jnp.broadcast_to(cand, (tq, V7X_LANES))
            cnt = count(lambda kk, off: jnp.where(kk == thr_b, lane128 + off, jnp.int32(2 ** 30)) < cb)
            return jnp.where(cnt <= need, cand, jcut)
        return lax.fori_loop(0, 15, body, jnp.zeros((tq, 1), I32))

    no_cut = jnp.full((tq, 1), 2 ** 15, I32)
    any_tie = jnp.max(jnp.where(tie_rows, 1, 0)) > 0
    jcut = lax.cond(any_tie, lambda: jnp.where(tie_rows, tie_cut(), no_cut), lambda: no_cut)

    rows = (H_C // KVH_C) * tq
    qg = [q_ref[0, g * (H_C // KVH_C):(g + 1) * (H_C // KVH_C)].reshape(rows, HD_C) * (HD_C ** -0.5)
          for g in range(KVH_C)]
    qg = [x.astype(BF16) for x in qg]

    def att_tile(j, carry):
        kk = key_ref[j]
        idx = lane + j * lt
        sel = ((kk > thr) | ((kk == thr) & (idx < jcut))) & (kk != INT_MIN)
        bias = jnp.where(sel, 0.0, -jnp.inf)
        bias4 = jnp.concatenate([bias] * (H_C // KVH_C), axis=0)
        v_t = v_ref[0, pl.ds(pl.multiple_of(j * lt, lt), lt), :]
        new = []
        for g in range(KVH_C):
            m_run, l_run, acc = carry[g]
            logits = jnp.dot(qg[g], kt_ref[0, j, g * HD_C:(g + 1) * HD_C, :], preferred_element_type=F32)
            logits = logits + bias4
            m_new = jnp.maximum(m_run, jnp.max(logits, axis=1, keepdims=True))
            m_safe = jnp.where(m_new == -jnp.inf, 0.0, m_new)
            p = jnp.exp(logits - m_safe)
            alpha = jnp.exp(m_run - m_safe)
            l_new = alpha * l_run + jnp.sum(p, axis=1, keepdims=True)
            acc_new = alpha * acc + jnp.dot(p.astype(BF16), v_t, preferred_element_type=F32)
            new.append((m_new, l_new, acc_new))
        return tuple(new)

    init = tuple((jnp.full((rows, 1), -jnp.inf, F32), jnp.zeros((rows, 1), F32),
                  jnp.zeros((rows, KVH_C * HD_C), F32)) for _ in range(KVH_C))
    fin = lax.fori_loop(0, nt, att_tile, init)
    for g in range(KVH_C):
        _, l_run, acc = fin[g]
        og = acc / l_run
        for hh in range(H_C // KVH_C):
            o_ref[0, g * (H_C // KVH_C) + hh] = og[hh * tq:(hh + 1) * tq, g * HD_C:(g + 1) * HD_C]


def _dsa(q_r, qi_r, wi_r, kt4, v_all, kit4, pos0, l_true):
    b, _, t, _ = q_r.shape
    n_tiles = kt4.shape[1]
    lp = n_tiles * KEY_TILE
    tq = _tile(t, 128)
    topk = min(TOPK_MAX, l_true // 4)
    kern = functools.partial(_dsa_kernel, tq=tq, topk=topk, pos0=pos0, l_true=l_true, n_tiles_max=n_tiles)
    return pl.pallas_call(
        kern,
        grid=(b, t // tq),
        in_specs=[
            pl.BlockSpec((1, H_C, tq, HD_C), lambda bi, i: (bi, 0, i, 0)),
            pl.BlockSpec((1, H_IDX, tq, D_IDX), lambda bi, i: (bi, 0, i, 0)),
            pl.BlockSpec((1, H_IDX, tq, 1), lambda bi, i: (bi, 0, i, 0)),
            pl.BlockSpec((1, n_tiles, KVH_C * HD_C, KEY_TILE), lambda bi, i: (bi, 0, 0, 0)),
            pl.BlockSpec((1, lp, KVH_C * HD_C), lambda bi, i: (bi, 0, 0)),
            pl.BlockSpec((1, n_tiles, D_IDX, KEY_TILE), lambda bi, i: (bi, 0, 0, 0)),
        ],
        out_specs=pl.BlockSpec((1, H_C, tq, HD_C), lambda bi, i: (bi, 0, i, 0)),
        out_shape=jax.ShapeDtypeStruct((b, H_C, t, HD_C), F32),
        scratch_shapes=[pltpu.VMEM((n_tiles, tq, KEY_TILE), I32)],
        compiler_params=_params(("arbitrary", "arbitrary")),
        name="dsa_attn",
    )(q_r, qi_r, wi_r, kt4, v_all, kit4)


def _outproj_kernel(x_ref, mod_ref, oa_ref, hb_ref, oc_ref, w_ref, g_ref, b_ref, o_ref):
    mix = (_dot(oa_ref[0], w_ref[0:W_A, :]) + _dot(hb_ref[0], w_ref[W_A:W_A + W_B, :])
           + _dot(oc_ref[0], w_ref[W_A + W_B:, :]))
    y = ALPHA * x_ref[0] + (1.0 + mod_ref[0, 2:3, :]) * mix
    o_ref[0] = _layer_norm(y, g_ref[...], b_ref[...])


def _outproj(x, mod, oa, hb, oc, w_out_bf, ln_g, ln_b):
    b, t, d = x.shape
    tm = _tile(t, 512)
    row = lambda bi, i: (bi, i, 0)
    mix_w = W_A + W_B + W_C
    return pl.pallas_call(
        _outproj_kernel,
        grid=(b, t // tm),
        in_specs=[
            pl.BlockSpec((1, tm, d), row),
            pl.BlockSpec((1, 6, d), lambda bi, i: (bi, 0, 0)),
            pl.BlockSpec((1, tm, W_A), row), pl.BlockSpec((1, tm, W_B), row), pl.BlockSpec((1, tm, W_C), row),
            pl.BlockSpec((mix_w, d), lambda bi, i: (0, 0)),
            pl.BlockSpec((1, d), lambda bi, i: (0, 0)), pl.BlockSpec((1, d), lambda bi, i: (0, 0)),
        ],
        out_specs=pl.BlockSpec((1, tm, d), row),
        out_shape=jax.ShapeDtypeStruct((b, t, d), F32),
        compiler_params=_params(("arbitrary", "arbitrary")),
        name="out_proj",
    )(x, mod, oa, hb, oc, w_out_bf, ln_g, ln_b)


def _route_t(s_t, sb_t):
    per = N_EXPERTS // N_GROUPS
    gsc = []
    for gi in range(N_GROUPS):
        a, b, c, d = sb_t[gi * per:(gi + 1) * per]
        m1, n1 = jnp.maximum(a, b), jnp.minimum(a, b)
        m2, n2 = jnp.maximum(c, d), jnp.minimum(c, d)
        gsc.append(jnp.maximum(m1, m2) + jnp.maximum(jnp.minimum(m1, m2), jnp.maximum(n1, n2)))
    best, g_sel = gsc[0], jnp.zeros_like(gsc[0], dtype=I32)
    for gi in range(1, N_GROUPS):
        better = gsc[gi] > best
        best = jnp.where(better, gsc[gi], best)
        g_sel = jnp.where(better, gi, g_sel)
    cand = [jnp.where(g_sel == (e // per), sb_t[e], -jnp.inf) for e in range(N_EXPERTS)]
    v1, e1 = cand[0], jnp.zeros_like(g_sel)
    for e in range(1, N_EXPERTS):
        better = cand[e] > v1
        v1 = jnp.where(better, cand[e], v1)
        e1 = jnp.where(better, e, e1)
    v2, e2 = jnp.full_like(v1, -jnp.inf), jnp.full_like(e1, -1)
    for e in range(N_EXPERTS):
        better = (cand[e] > v2) & (e1 != e)
        v2 = jnp.where(better, cand[e], v2)
        e2 = jnp.where(better, e, e2)
    s1 = jnp.zeros_like(v1)
    s2 = jnp.zeros_like(v1)
    for e in range(N_EXPERTS):
        s1 = jnp.where(e1 == e, s_t[e], s1)
        s2 = jnp.where(e2 == e, s_t[e], s2)
    tot = s1 + s2
    return [jnp.where(e1 == e, s1 / tot, jnp.where(e2 == e, s2 / tot, 0.0)) for e in range(N_EXPERTS)]


def _moe_kernel(x_ref, mod_ref, wr_ref, br_ref, wg_ref, wu_ref, wd_ref, g_ref, b_ref, o_ref,
                u_ref, gate_ref, acc_ref):
    e = pl.program_id(2)
    tm = x_ref.shape[1]

    @pl.when(e == 0)
    def _():
        u = x_ref[0] * (1.0 + mod_ref[0, 4:5, :]) + mod_ref[0, 3:4, :]
        u_ref[...] = u.astype(BF16)
        s = _sigmoid(_dot_exact(u, wr_ref[...]))
        sb = s + br_ref[...]
        s_t, sb_t = s.T, sb.T
        rows = _route_t([s_t[k:k + 1, :] for k in range(N_EXPERTS)],
                        [sb_t[k:k + 1, :] for k in range(N_EXPERTS)])
        row_id = lax.broadcasted_iota(I32, (N_EXPERTS, tm), 0)
        g16 = jnp.zeros((N_EXPERTS, tm), F32)
        for k in range(N_EXPERTS):
            g16 = jnp.where(row_id == k, rows[k], g16)
        g_t = jnp.concatenate([g16, jnp.zeros((V7X_LANES - N_EXPERTS, tm), F32)], axis=0)
        gate_ref[...] = g_t.T
        acc_ref[...] = jnp.zeros_like(acc_ref)

    u = u_ref[...]
    h = _silu(jnp.dot(u, wg_ref[0], preferred_element_type=F32)) * jnp.dot(u, wu_ref[0], preferred_element_type=F32)
    lane = lax.broadcasted_iota(I32, (tm, V7X_LANES), 1)
    gate = jnp.sum(jnp.where(lane == e, gate_ref[...], 0.0), axis=1, keepdims=True)
    acc_ref[...] += gate * jnp.dot(h.astype(BF16), wd_ref[0], preferred_element_type=F32)

    @pl.when(e == N_EXPERTS - 1)
    def _():
        y = ALPHA * x_ref[0] + (1.0 + mod_ref[0, 5:6, :]) * acc_ref[...]
        o_ref[0] = _layer_norm(y, g_ref[...], b_ref[...])


def _moe(x, mod, wr_p, br_p, wg_bf, wu_bf, wd_bf, ln_g, ln_b):
    b, t, d = x.shape
    tm = _tile(t, 1024)
    row = lambda bi, i, e: (bi, i, 0)
    const2 = lambda bi, i, e: (0, 0)
    return pl.pallas_call(
        _moe_kernel,
        grid=(b, t // tm, N_EXPERTS),
        in_specs=[
            pl.BlockSpec((1, tm, d), row),
            pl.BlockSpec((1, 6, d), lambda bi, i, e: (bi, 0, 0)),
            pl.BlockSpec((d, V7X_LANES), const2), pl.BlockSpec((1, V7X_LANES), const2),
            pl.BlockSpec((1, d, D_FF_E), lambda bi, i, e: (e, 0, 0)),
            pl.BlockSpec((1, d, D_FF_E), lambda bi, i, e: (e, 0, 0)),
            pl.BlockSpec((1, D_FF_E, d), lambda bi, i, e: (e, 0, 0)),
            pl.BlockSpec((1, d), const2), pl.BlockSpec((1, d), const2),
        ],
        out_specs=pl.BlockSpec((1, tm, d), row),
        out_shape=jax.ShapeDtypeStruct((b, t, d), F32),
        scratch_shapes=[pltpu.VMEM((tm, d), BF16), pltpu.VMEM((tm, V7X_LANES), F32), pltpu.VMEM((tm, d), F32)],
        compiler_params=_params(("arbitrary", "arbitrary", "arbitrary")),
        name="moe_ffn",
    )(x, mod, wr_p, br_p, wg_bf, wu_bf, wd_bf, ln_g, ln_b)


def _proj_columns():
    sizes = [H_A * DK_A, H_A * DK_A, W_A, W_A, GATE_RANK, 2 * W_B, W_B, W_B, H_B, H_B,
             W_C, KVH_C * HD_C, KVH_C * HD_C, H_IDX * D_IDX, D_IDX, H_IDX]
    offs = np.concatenate([[0], np.cumsum(sizes)])
    p = int(offs[-1])
    seg = lambda k: np.arange(offs[k], offs[k + 1])
    pad = lambda n: np.full((n,), p)
    qa, ka, va, ga, gr, qk, vb, ob, ib, fb, qc, kc, vc, qi, ki, wi = [seg(k) for k in range(16)]
    cols = np.concatenate([
        qa, ka, va, ga, gr, pad(128 - GATE_RANK),
        qk, vb, ob, ib, fb, pad(128 - 2 * H_B),
        qc, kc, vc, qi, ki, wi, pad(128 - D_IDX - H_IDX)])
    assert cols.shape[0] == P_TOTAL
    return cols, p


def _rope_tables(pos):
    def pattern(hd):
        half = hd // 2
        inv = jnp.power(ROPE_THETA, -jnp.arange(half, dtype=F32) / half)
        ang = pos.astype(F32)[:, None] * inv[None, :]
        cos = jnp.tile(jnp.concatenate([jnp.cos(ang), jnp.cos(ang)], -1), (1, V7X_LANES // hd))
        sin = jnp.tile(jnp.concatenate([-jnp.sin(ang), jnp.sin(ang)], -1), (1, V7X_LANES // hd))
        return cos, sin
    c64, s64 = pattern(HD_C)
    c32, s32 = pattern(D_IDX)
    keep = (jnp.arange(V7X_LANES) < D_IDX)[None, :]
    cm, sm = jnp.where(keep, c32, 1.0), jnp.where(keep, s32, 0.0)
    return (jnp.concatenate([c64, s64], -1), jnp.concatenate([c32, s32], -1), jnp.concatenate([cm, sm], -1))


def _pad_keys(a, lp):
    return jnp.pad(a, ((0, 0), (0, lp - a.shape[1]), (0, 0)))


def _layer(x, mod, lw, shared, tabs, pos0, past, first):
    b, t, d = x.shape
    consts = shared["consts"]
    res = _inproj(x, mod, shared["ln_in_g"], shared["ln_in_b"], lw["w_p"], lw["bias_p"], tabs, first)
    if first:
        x, pa, pb, pc = res
    else:
        pa, pb, pc = res

    if past is None:
        s0_t = jnp.zeros((b, H_A, DV_A, DK_A), F32)
    else:
        s0_t = jnp.swapaxes(past[3], -1, -2)
    oa, sfin_t = _gla(pa, s0_t, lw["wup_p"], lw["bup"], lw["gla_g"], consts)
    s_new = jnp.swapaxes(sfin_t, -1, -2)

    if past is None:
        conv0 = jnp.zeros((b, 8, 2 * W_B), F32)
        c0_st = jnp.zeros((b, W_B, MST_W), F32)
        m0 = jnp.zeros((b, 8, V7X_LANES), F32)
    else:
        conv0 = jnp.pad(past[7], ((0, 0), (8 - (CONV_W - 1), 0), (0, 0)))
        eye = jnp.eye(H_B, dtype=F32)
        c_bd = jnp.einsum('bhkv,hg->bhkgv', past[4], eye).reshape(b, W_B, W_B)
        n_bd = jnp.einsum('bhk,hg->bhkg', past[5], eye).reshape(b, W_B, H_B)
        c0_st = jnp.concatenate([c_bd, n_bd, jnp.zeros((b, W_B, V7X_LANES - H_B), F32)], -1)
        m0 = jnp.pad(jnp.broadcast_to(past[6][:, :, None], (b, H_B, V7X_LANES)), ((0, 0), (0, 8 - H_B), (0, 0)))
    hb, convo, cfin, mfin = _mlstm(pb, conv0, lw["cw_p"], lw["cb"], c0_st, m0, lw["mlstm_g"], consts)
    conv_new = convo[:, 8 - (CONV_W - 1):, :]
    cf = cfin[:, :, :W_B].reshape(b, H_B, DH_B, H_B, DH_B)
    c_new = jnp.stack([cf[:, h, :, h, :] for h in range(H_B)], axis=1)
    nf = cfin[:, :, W_B:W_B + H_B].reshape(b, H_B, DH_B, H_B)
    n_new = jnp.stack([nf[:, h, :, h] for h in range(H_B)], axis=1)
    m_new = mfin[:, :H_B, 0]

    k_new = pc[:, :, 512:640]
    v_new = pc[:, :, 640:768]
    ki_new = pc[:, :, 1024:1024 + D_IDX]
    if past is None:
        k_all, v_all, ki_all = k_new, v_new, ki_new
    else:
        k_all = jnp.concatenate([past[0].reshape(b, -1, KVH_C * HD_C), k_new], axis=1)
        v_all = jnp.concatenate([past[1].reshape(b, -1, KVH_C * HD_C), v_new], axis=1)
        ki_all = jnp.concatenate([past[2], ki_new], axis=1)
    l_true = k_all.shape[1]
    n_tiles = -(-l_true // KEY_TILE)
    lp = n_tiles * KEY_TILE
    kt4 = _pad_keys(k_all, lp).astype(BF16).reshape(b, n_tiles, KEY_TILE, -1).swapaxes(-1, -2)
    kit4 = _pad_keys(ki_all, lp).astype(BF16).reshape(b, n_tiles, KEY_TILE, -1).swapaxes(-1, -2)
    v_bf = _pad_keys(v_all, lp).astype(BF16)
    q_r = pc[:, :, 0:512].reshape(b, t, H_C, HD_C).swapaxes(1, 2)
    qi_r = pc[:, :, 768:1024].reshape(b, t, H_IDX, D_IDX).swapaxes(1, 2)
    wi_r = pc[:, :, 1024 + D_IDX:1024 + D_IDX + H_IDX].swapaxes(1, 2)[..., None]
    oc_r = _dsa(q_r, qi_r, wi_r, kt4, v_bf, kit4, pos0, l_true)
    oc = oc_r.swapaxes(1, 2).reshape(b, t, W_C)

    x1 = _outproj(x, mod, oa, hb, oc, lw["w_out"], lw["ln1_g"], lw["ln1_b"])
    x2 = _moe(x1, mod, shared["wr_p"], shared["br_p"], lw["wg"], lw["wu"], lw["wd"], lw["ln2_g"], lw["ln2_b"])
    state = (k_new.reshape(b, t, KVH_C, HD_C), v_new.reshape(b, t, KVH_C, HD_C), ki_new,
             s_new, c_new, n_new, m_new, conv_new)
    return x2, state


def kernel(x_prompt, x_sample, c_prompt, c_sample, cache_k, cache_v, cache_kidx, state_gla, state_mlstm_C,
           state_mlstm_n, state_mlstm_m, state_mlstm_conv, ln_in_g, ln_in_b, w_ada, b_ada, w_in, b_in, gla_w_up,
           gla_b_up, gla_norm_g, mlstm_conv_w, mlstm_conv_b, mlstm_norm_g, w_out, ln1_g, ln1_b, w_router,
           b_router, w_gate, w_up, w_down, ln2_g, ln2_b):
    depth = w_in.shape[0]
    d = x_prompt.shape[-1]
    bp, tp, _ = x_prompt.shape
    bs, ts, _ = x_sample.shape
    past_len = cache_k.shape[2]

    cols, p = _proj_columns()
    tril = jnp.tril(jnp.ones((CHUNK, CHUNK), F32))
    ones_bd = (jnp.arange(W_A)[:, None] // DV_A == jnp.arange(W_A)[None, :] // DV_A).astype(BF16)
    shared = dict(
        consts=(tril, ones_bd),
        ln_in_g=ln_in_g.reshape(1, d), ln_in_b=ln_in_b.reshape(1, d),
        wr_p=jnp.pad(w_router, ((0, 0), (0, V7X_LANES - N_EXPERTS))),
        br_p=jnp.pad(b_router, (0, V7X_LANES - N_EXPERTS)).reshape(1, V7X_LANES),
    )
    layers = []
    for l in range(depth):
        w_ext = jnp.concatenate([w_in[l], jnp.zeros((d, 1), F32)], axis=1)
        b_ext = jnp.concatenate([b_in[l], jnp.zeros((1,), F32)])
        layers.append(dict(
            w_p=w_ext[:, cols].astype(BF16), bias_p=b_ext[cols].reshape(1, P_TOTAL),
            wup_p=jnp.pad(gla_w_up[l], ((0, V7X_LANES - GATE_RANK), (0, 0))).astype(BF16),
            bup=gla_b_up[l].reshape(1, -1), gla_g=gla_norm_g[l].reshape(1, -1),
            cw_p=jnp.pad(mlstm_conv_w[l], ((0, 8 - CONV_W), (0, 0))), cb=mlstm_conv_b[l].reshape(1, -1),
            mlstm_g=mlstm_norm_g[l].reshape(1, -1),
            w_out=w_out[l].astype(BF16), ln1_g=ln1_g[l].reshape(1, d), ln1_b=ln1_b[l].reshape(1, d),
            wg=w_gate[l].astype(BF16), wu=w_up[l].astype(BF16), wd=w_down[l].astype(BF16),
            ln2_g=ln2_g[l].reshape(1, d), ln2_b=ln2_b[l].reshape(1, d),
        ))

    mod_all = _ada(jnp.concatenate([c_prompt, c_sample], axis=0), w_ada, b_ada)
    mod_all = mod_all.reshape(depth, bp + bs, 6, d)
    tabs_p = _rope_tables(jnp.arange(tp))
    tabs_s = _rope_tables(past_len + jnp.arange(ts))

    xp, xs = x_prompt, x_sample
    p_states, s_states = [], []
    for l in range(depth):
        xp, st_p = _layer(xp, mod_all[l, :bp], layers[l], shared, tabs_p, 0, None, l == 0)
        p_states.append(st_p)
        past = (cache_k[l], cache_v[l], cache_kidx[l], state_gla[l], state_mlstm_C[l], state_mlstm_n[l],
                state_mlstm_m[l], state_mlstm_conv[l])
        xs, st_s = _layer(xs, mod_all[l, bp:], layers[l], shared, tabs_s, past_len, past, l == 0)
        s_states.append(st_s)

    stk = lambda states, k: jnp.stack([st[k] for st in states], axis=0)
    return ((xp, xs) + tuple(stk(p_states, k) for k in range(8)) + tuple(stk(s_states, k) for k in range(8)))
```

```python
import functools

import numpy as np
import jax
import jax.numpy as jnp
from jax import lax
from jax.experimental import pallas as pl
from jax.experimental.pallas import tpu as pltpu

F32 = jnp.float32
BF16 = jnp.bfloat16
I32 = jnp.int32
HIGHEST = lax.Precision.HIGHEST

CHUNK = 64
H_A, DK_A, DV_A = 4, 32, 64
W_A = H_A * DV_A
GATE_RANK = 16
GLA_TAU = 16.0
H_B, DH_B = 4, 64
W_B = H_B * DH_B
CONV_W = 4
H_C, KVH_C, HD_C = 8, 2, 64
W_C = H_C * HD_C
H_IDX, D_IDX = 8, 32
TOPK_MAX = 256
N_EXPERTS, N_GROUPS = 16, 4
D_FF_E = 256
ROPE_THETA = 10000.0
LN_EPS = 1e-5
DEPTH = 2
ALPHA = (2 * DEPTH) ** 0.25

V7X_LANES = 128
V7X_VMEM_BYTES = 64 * 1024 * 1024
VMEM_LIMIT = (V7X_VMEM_BYTES * 3) // 4

PA_W = 896
PB_W = 1152
PC_W = 1152
P_TOTAL = PA_W + PB_W + PC_W

KEY_TILE_MAX = 2048
INT_MIN = -2 ** 31
I16 = jnp.int16
I16_MIN, I16_MAX = -2 ** 15, 2 ** 15 - 1
V7X_MXU_DEPTH = 256
MXU_DEPTH = V7X_MXU_DEPTH
LOG2_E = 1.4426950408889634
MASK_BIAS = -1e30


def _tile(n, pref):
    t = min(n, pref)
    while n % t:
        t //= 2
    return t


def _params(sem):
    return pltpu.CompilerParams(dimension_semantics=sem, vmem_limit_bytes=VMEM_LIMIT)


def _dot(a, b):
    return jnp.dot(a.astype(BF16), b.astype(BF16), preferred_element_type=F32)


def _dot_nt(a, b):
    return lax.dot_general(a.astype(BF16), b.astype(BF16), (((1,), (1,)), ((), ())), preferred_element_type=F32)


def _dot_tn(a, b):
    return lax.dot_general(a.astype(BF16), b.astype(BF16), (((0,), (0,)), ((), ())), preferred_element_type=F32)


def _dot_exact(a, b):
    return jnp.dot(a, b, preferred_element_type=F32, precision=HIGHEST)


def _log_sigmoid(x):
    return jnp.minimum(x, 0.0) - jnp.log1p(jnp.exp(-jnp.abs(x)))


def _sigmoid(x):
    return 1.0 / (1.0 + jnp.exp(-x))


def _silu(x):
    return x * _sigmoid(x)


def _layer_norm(x, g, b):
    xc = x - jnp.mean(x, axis=-1, keepdims=True)
    var = jnp.mean(xc * xc, axis=-1, keepdims=True)
    return xc * lax.rsqrt(var + LN_EPS) * g + b


def _ada_kernel(c_ref, w_ref, b_ref, o_ref):
    o_ref[0] = _dot(_silu(c_ref[...]), w_ref[0]) + b_ref[0]


def _ada(c_all, w_ada, b_ada):
    depth, d, n6 = w_ada.shape
    nb = c_all.shape[0]
    tn = 1024
    return pl.pallas_call(
        _ada_kernel,
        grid=(depth, n6 // tn),
        in_specs=[
            pl.BlockSpec((nb, d), lambda l, j: (0, 0)),
            pl.BlockSpec((1, d, tn), lambda l, j: (l, 0, j)),
            pl.BlockSpec((1, 1, tn), lambda l, j: (l, 0, j)),
        ],
        out_specs=pl.BlockSpec((1, nb, tn), lambda l, j: (l, 0, j)),
        out_shape=jax.ShapeDtypeStruct((depth, nb, n6), F32),
        compiler_params=_params(("arbitrary", "arbitrary")),
        name="ada_mod",
    )(c_all, w_ada, b_ada.reshape(depth, 1, n6))


def _rot_half(x, half):
    lane = lax.broadcasted_iota(I32, x.shape, 1)
    first = (lane % (2 * half)) < half
    return jnp.where(first, pltpu.roll(x, V7X_LANES - half, 1), pltpu.roll(x, half, 1))


def _row_blocks(b, t, rows):
    tm = _tile(t, rows)
    bb = _tile(b, max(1, rows // tm)) if tm == t else 1
    return bb, tm


def _inproj_kernel(x_ref, mod_ref, g_ref, b_ref, w_ref, bias_ref, t64_ref, t32_ref, tm_ref,
                   *out_refs, apply_ln):
    if apply_ln:
        xln_ref, pa_ref, pb_ref, pc_ref = out_refs
    else:
        pa_ref, pb_ref, pc_ref = out_refs
    bb, tm, d = x_ref.shape
    x = x_ref[...]
    if apply_ln:
        x = _layer_norm(x, g_ref[...], b_ref[...])
        xln_ref[...] = x
    u = (x * (1.0 + mod_ref[:, 1:2, :]) + mod_ref[:, 0:1, :]).reshape(bb * tm, d).astype(BF16)
    pa = jnp.dot(u, w_ref[:, 0:PA_W], preferred_element_type=F32) + bias_ref[:, 0:PA_W]
    pa_ref[...] = pa.reshape(bb, tm, PA_W)
    pb = (jnp.dot(u, w_ref[:, PA_W:PA_W + PB_W], preferred_element_type=F32)
          + bias_ref[:, PA_W:PA_W + PB_W])
    pb_ref[...] = pb.reshape(bb, tm, PB_W)
    pc = (jnp.dot(u, w_ref[:, PA_W + PB_W:P_TOTAL], preferred_element_type=F32)
          + bias_ref[:, PA_W + PB_W:P_TOTAL])
    c64, s64 = t64_ref[:, 0:128], t64_ref[:, 128:256]
    c32, s32 = t32_ref[:, 0:128], t32_ref[:, 128:256]
    cm, sm = tm_ref[:, 0:128], tm_ref[:, 128:256]

    def put(s, val):
        pc_ref[:, :, s * 128:(s + 1) * 128] = val.reshape(bb, tm, 128)

    for s in range(5):
        xs = pc[:, s * 128:(s + 1) * 128]
        put(s, xs * c64 + _rot_half(xs, HD_C // 2) * s64)
    put(5, pc[:, 640:768])
    for s in range(6, 8):
        xs = pc[:, s * 128:(s + 1) * 128]
        put(s, xs * c32 + _rot_half(xs, D_IDX // 2) * s32)
    xs = pc[:, 1024:1152]
    put(8, xs * cm + _rot_half(xs, D_IDX // 2) * sm)


def _inproj(x, mod, ln_g, ln_b, w_p, bias_p, tabs, apply_ln):
    b, t, d = x.shape
    bb, tm = _row_blocks(b, t, 512)
    tabs = [jnp.tile(tb, (bb, 1)) for tb in tabs]
    row = lambda bi, i: (bi, i, 0)
    out_shape = [jax.ShapeDtypeStruct((b, t, PA_W), F32), jax.ShapeDtypeStruct((b, t, PB_W), F32),
                 jax.ShapeDtypeStruct((b, t, PC_W), F32)]
    out_specs = [pl.BlockSpec((bb, tm, PA_W), row), pl.BlockSpec((bb, tm, PB_W), row),
                 pl.BlockSpec((bb, tm, PC_W), row)]
    if apply_ln:
        out_shape = [jax.ShapeDtypeStruct((b, t, d), F32)] + out_shape
        out_specs = [pl.BlockSpec((bb, tm, d), row)] + out_specs
    tab_spec = pl.BlockSpec((bb * tm, 256), lambda bi, i: (i, 0))
    return pl.pallas_call(
        functools.partial(_inproj_kernel, apply_ln=apply_ln),
        grid=(b // bb, t // tm),
        in_specs=[
            pl.BlockSpec((bb, tm, d), row),
            pl.BlockSpec((bb, 6, d), lambda bi, i: (bi, 0, 0)),
            pl.BlockSpec((1, d), lambda bi, i: (0, 0)),
            pl.BlockSpec((1, d), lambda bi, i: (0, 0)),
            pl.BlockSpec((d, P_TOTAL), lambda bi, i: (0, 0)),
            pl.BlockSpec((1, P_TOTAL), lambda bi, i: (0, 0)),
            tab_spec, tab_spec, tab_spec,
        ],
        out_specs=out_specs,
        out_shape=out_shape,
        compiler_params=_params(("arbitrary", "arbitrary")),
        name="in_proj",
    )(x, mod, ln_g, ln_b, w_p, bias_p, *tabs)


def _seg_sum(x, ones_bd):
    hi = x.astype(BF16)
    lo = (x - hi.astype(F32)).astype(BF16)
    return (jnp.dot(hi, ones_bd, preferred_element_type=F32)
            + jnp.dot(lo, ones_bd, preferred_element_type=F32))


def _gla_kernel(pa_ref, s0_ref, wup_ref, bup_ref, gn_ref, tril_ref, ones_ref, oa_ref, sfin_ref, st_ref, *, nc):
    i = pl.program_id(1)

    @pl.when(i == 0)
    def _():
        st_ref[...] = jnp.zeros_like(st_ref)
        for h in range(H_A):
            st_ref[h * DV_A:(h + 1) * DV_A, h * DK_A:(h + 1) * DK_A] = s0_ref[0, h]

    tril = tril_ref[...]
    ones_bd = ones_ref[...]
    r_sk = lax.broadcasted_iota(I32, (H_A * CHUNK, H_A * DK_A), 0) // CHUNK
    c_sk = lax.broadcasted_iota(I32, (H_A * CHUNK, H_A * DK_A), 1) // DK_A
    mask_sk = r_sk == c_sk
    r_sv = lax.broadcasted_iota(I32, (H_A * CHUNK, W_A), 0) // CHUNK
    c_sv = lax.broadcasted_iota(I32, (H_A * CHUNK, W_A), 1) // DV_A
    mask_sv = r_sv == c_sv
    r_st = lax.broadcasted_iota(I32, (W_A, H_A * DK_A), 0) // DV_A
    c_st = lax.broadcasted_iota(I32, (W_A, H_A * DK_A), 1) // DK_A
    mask_st = r_st == c_st
    t_i = lax.broadcasted_iota(I32, (CHUNK, H_A * CHUNK), 0)
    s_i = lax.broadcasted_iota(I32, (CHUNK, H_A * CHUNK), 1) % CHUNK
    causal = t_i >= s_i

    st = st_ref[...]
    for c in range(nc):
        r = slice(c * CHUNK, (c + 1) * CHUNK)
        q = pa_ref[0, r, 0:128] * (DK_A ** -0.5)
        k = pa_ref[0, r, 128:256]
        v = pa_ref[0, r, 256:512]
        g = pa_ref[0, r, 512:768]
        misc = pa_ref[0, r, 768:896]
        pre = _dot(misc, wup_ref[...]) + bup_ref[...]
        log_a = _log_sigmoid(pre) * (1.0 / GLA_TAU)
        bc = _dot_exact(tril, log_a)
        bm = bc[CHUNK // 2 - 1:CHUNK // 2, :]
        bl = bc[CHUNK - 1:CHUNK, :]
        inter = _dot_nt(q * jnp.exp(bc), st)
        qs = q * jnp.exp(bc - bm)
        ks = k * jnp.exp(bm - bc)
        ks_bd = jnp.where(mask_sk, jnp.concatenate([ks] * H_A, axis=0), 0.0)
        att = jnp.where(causal, _dot_nt(qs, ks_bd), 0.0)
        v_bd = jnp.where(mask_sv, jnp.concatenate([v] * H_A, axis=0), 0.0)
        o = inter + _dot(att, v_bd)
        ms = _seg_sum(o * o, ones_bd) * (1.0 / DV_A)
        oa_ref[0, r, :] = o * lax.rsqrt(ms + LN_EPS) * gn_ref[...] * _silu(g)
        upd = _dot_tn(v, k * jnp.exp(bl - bc))
        st = jnp.exp(bl) * st + jnp.where(mask_st, upd, 0.0)
    st_ref[...] = st
    for h in range(H_A):
        sfin_ref[0, h] = st[h * DV_A:(h + 1) * DV_A, h * DK_A:(h + 1) * DK_A]


def _gla(pa, s0_t, wup_p, bup, gnorm, consts):
    b, t, _ = pa.shape
    tb = _tile(t, 512)
    tril, ones_bd = consts
    full = lambda shape: pl.BlockSpec(shape, lambda bi, i: (0,) * len(shape))
    return pl.pallas_call(
        functools.partial(_gla_kernel, nc=tb // CHUNK),
        grid=(b, t // tb),
        in_specs=[
            pl.BlockSpec((1, tb, PA_W), lambda bi, i: (bi, i, 0)),
            pl.BlockSpec((1, H_A, DV_A, DK_A), lambda bi, i: (bi, 0, 0, 0)),
            full((128, 128)), full((1, 128)), full((1, W_A)), full((CHUNK, CHUNK)), full((W_A, W_A)),
        ],
        out_specs=[
            pl.BlockSpec((1, tb, W_A), lambda bi, i: (bi, i, 0)),
            pl.BlockSpec((1, H_A, DV_A, DK_A), lambda bi, i: (bi, 0, 0, 0)),
        ],
        out_shape=[jax.ShapeDtypeStruct((b, t, W_A), F32),
                   jax.ShapeDtypeStruct((b, H_A, DV_A, DK_A), F32)],
        scratch_shapes=[pltpu.VMEM((W_A, H_A * DK_A), F32)],
        compiler_params=_params(("arbitrary", "arbitrary")),
        name="gla_scan",
    )(pa, s0_t, wup_p, bup, gnorm, tril, ones_bd)


MST_W = W_B + V7X_LANES


def _expand_heads(cols, lane_head):
    out = jnp.zeros(lane_head.shape, F32)
    for h, col in enumerate(cols):
        out = jnp.where(lane_head == h, col, out)
    return out


def _mlstm_kernel(pb_ref, conv0_ref, cw_ref, cb_ref, c0_ref, m0_ref, gn_ref, tril_ref, ones_ref,
                  hb_ref, convo_ref, cfin_ref, mfin_ref, xs_ref, cst_ref, m_ref, *, nc):
    i = pl.program_id(1)
    tb = nc * CHUNK

    @pl.when(i == 0)
    def _():
        xs_ref[0:8, :] = conv0_ref[0]
        cst_ref[...] = c0_ref[0]
        m_ref[...] = m0_ref[0]

    xs_ref[8:8 + tb, :] = pb_ref[0, :, 0:512]
    tril = tril_ref[...]
    ones_bd = ones_ref[...]

    lane_h = lax.broadcasted_iota(I32, (CHUNK, W_B), 1) // DH_B
    t_i = lax.broadcasted_iota(I32, (CHUNK, W_B), 0)
    s_i = lax.broadcasted_iota(I32, (CHUNK, W_B), 1) % CHUNK
    causal = t_i >= s_i
    diag = t_i == s_i
    e_r = lax.broadcasted_iota(I32, (V7X_LANES, W_B), 0)
    e_c = lax.broadcasted_iota(I32, (V7X_LANES, W_B), 1) // DH_B
    e_f = (e_r == e_c + H_B).astype(F32)
    e_i = (e_r == e_c).astype(F32)
    bd_r = lax.broadcasted_iota(I32, (W_B, W_B), 0) // DH_B
    bd_c = lax.broadcasted_iota(I32, (W_B, W_B), 1) // DH_B
    mask_bd = bd_r == bd_c
    st_lane = lax.broadcasted_iota(I32, (CHUNK, MST_W), 1)
    st_lane_h = jnp.where(st_lane < W_B, st_lane // DH_B, jnp.where(st_lane < W_B + H_B, st_lane - W_B, -1))
    row_h = lax.broadcasted_iota(I32, (W_B, MST_W), 0) // DH_B
    col = lax.broadcasted_iota(I32, (W_B, MST_W), 1)
    col_h = jnp.where(col < W_B, col // DH_B, jnp.where(col < W_B + H_B, col - W_B, -1))
    mask_state = row_h == col_h
    ones_lane = (lax.broadcasted_iota(I32, (CHUNK, V7X_LANES), 1) < H_B).astype(F32)
    ones_sq = jnp.ones((CHUNK, CHUNK), F32)

    cst = cst_ref[...]
    m_prev = [m_ref[h:h + 1, 0:1] for h in range(H_B)]
    for c in range(nc):
        base = 8 + c * CHUNK
        r = slice(c * CHUNK, (c + 1) * CHUNK)
        conv = cb_ref[...]
        for j in range(CONV_W):
            conv = conv + xs_ref[base - (CONV_W - 1) + j:base - (CONV_W - 1) + j + CHUNK, :] * cw_ref[j:j + 1, :]
        act = _silu(conv)
        q = act[:, 0:W_B]
        k = act[:, W_B:2 * W_B] * (DH_B ** -0.5)
        v = pb_ref[0, r, 512:768]
        og = pb_ref[0, r, 768:1024]
        gates = pb_ref[0, r, 1024:1152]
        bcum = _dot_exact(tril, _log_sigmoid(gates))
        b_exp = _dot_exact(bcum, e_f)
        i_exp = _dot_exact(gates, e_i)
        row_term = _dot_exact(ones_sq, jnp.where(diag, i_exp - b_exp, 0.0))
        dmat = jnp.where(causal, b_exp + row_term, -jnp.inf)
        bcol = [bcum[:, H_B + h:H_B + h + 1] for h in range(H_B)]
        icol = [gates[:, h:h + 1] for h in range(H_B)]
        ginter = [bcol[h] + m_prev[h] for h in range(H_B)]
        mt = [jnp.maximum(ginter[h], jnp.max(jnp.where(lane_h == h, dmat, -jnp.inf), axis=1, keepdims=True))
              for h in range(H_B)]
        mt_exp = _expand_heads(mt, lane_h)
        k_bd = jnp.where(mask_bd, jnp.concatenate([k] * H_B, axis=0), 0.0)
        qk = _dot_nt(q, k_bd) * jnp.exp(dmat - mt_exp)
        v_aug = jnp.concatenate([v, ones_lane], axis=1)
        v_bd = jnp.where(mask_state, jnp.concatenate([v_aug] * H_B, axis=0), 0.0)
        w_inter = _expand_heads([jnp.exp(ginter[h] - mt[h]) for h in range(H_B)], st_lane_h)
        tot = w_inter * _dot(q, cst) + _dot(qk, v_bd)
        num = tot[:, 0:W_B]
        den = _expand_heads([tot[:, W_B + h:W_B + h + 1] for h in range(H_B)], lane_h)
        hh = num / jnp.maximum(jnp.abs(den), jnp.exp(-mt_exp))
        mu = _seg_sum(hh, ones_bd) * (1.0 / DH_B)
        hc = hh - mu
        var = _seg_sum(hc * hc, ones_bd) * (1.0 / DH_B)
        hb_ref[0, r, :] = hc * lax.rsqrt(var + LN_EPS) * gn_ref[...] * _sigmoid(og)
        m_new = [mt[h][CHUNK - 1:CHUNK, :] for h in range(H_B)]
        b_last = [bcol[h][CHUNK - 1:CHUNK, :] for h in range(H_B)]
        w_c = [jnp.exp(b_last[h] + m_prev[h] - m_new[h]) for h in range(H_B)]
        w_s = _expand_heads([jnp.exp(b_last[h] - bcol[h] + icol[h] - m_new[h]) for h in range(H_B)], lane_h)
        upd = _dot_tn(k * w_s, v_aug)
        w_c_rows = jnp.zeros((W_B, MST_W), F32)
        for h in range(H_B):
            w_c_rows = jnp.where(row_h == h, w_c[h], w_c_rows)
        cst = w_c_rows * cst + jnp.where(mask_state, upd, 0.0)
        m_prev = m_new

    cst_ref[...] = cst
    for h in range(H_B):
        m_ref[h:h + 1, :] = jnp.broadcast_to(m_prev[h], (1, V7X_LANES))
    tail = xs_ref[tb:tb + 8, :]
    xs_ref[0:8, :] = tail
    convo_ref[0] = tail
    cfin_ref[0] = cst
    mfin_ref[0] = m_ref[...]


def _mlstm(pb, conv0_p, cw_p, cb, c0_st, m0_p, gnorm, consts):
    b, t, _ = pb.shape
    tb = _tile(t, 512)
    tril, ones_bd = consts
    full = lambda shape: pl.BlockSpec(shape, lambda bi, i: (0,) * len(shape))
    per_b = lambda shape: pl.BlockSpec((1,) + shape, lambda bi, i: (bi,) + (0,) * len(shape))
    return pl.pallas_call(
        functools.partial(_mlstm_kernel, nc=tb // CHUNK),
        grid=(b, t // tb),
        in_specs=[
            pl.BlockSpec((1, tb, PB_W), lambda bi, i: (bi, i, 0)),
            per_b((8, 2 * W_B)), full((8, 2 * W_B)), full((1, 2 * W_B)),
            per_b((W_B, MST_W)), per_b((8, V7X_LANES)), full((1, W_B)), full((CHUNK, CHUNK)), full((W_B, W_B)),
        ],
        out_specs=[
            pl.BlockSpec((1, tb, W_B), lambda bi, i: (bi, i, 0)),
            per_b((8, 2 * W_B)), per_b((W_B, MST_W)), per_b((8, V7X_LANES)),
        ],
        out_shape=[jax.ShapeDtypeStruct((b, t, W_B), F32),
                   jax.ShapeDtypeStruct((b, 8, 2 * W_B), F32),
                   jax.ShapeDtypeStruct((b, W_B, MST_W), F32),
                   jax.ShapeDtypeStruct((b, 8, V7X_LANES), F32)],
        scratch_shapes=[pltpu.VMEM((tb + 8, 2 * W_B), F32), pltpu.VMEM((W_B, MST_W), F32),
                        pltpu.VMEM((8, V7X_LANES), F32)],
        compiler_params=_params(("arbitrary", "arbitrary")),
        name="mlstm_scan",
    )(pb, conv0_p, cw_p, cb, c0_st, m0_p, gnorm, tril, ones_bd)


def _dsa_kernel(q_ref, qi_ref, wi_ref, kt_ref, v_ref, kit_ref, o_ref, hi_ref, lo_ref,
                *, tq, topk, pos0, l_true, lt):
    qb = pl.program_id(1)
    n_slab = lt // V7X_LANES
    heads_per_kv = H_C // KVH_C
    t_row = lax.broadcasted_iota(I32, (tq, 1), 0) + qb * tq + pos0
    limit = jnp.minimum((t_row // CHUNK + 1) * CHUNK, l_true)
    last_limit = jnp.minimum(((qb * tq + tq - 1 + pos0) // CHUNK + 1) * CHUNK, l_true)
    nt = (last_limit + lt - 1) // lt
    lane = lax.broadcasted_iota(I32, (tq, lt), 1)

    qi = qi_ref[0].reshape(H_IDX * tq, D_IDX).astype(BF16)
    wcol = wi_ref[0].reshape(H_IDX * tq, 1) * (H_IDX ** -0.5 * D_IDX ** -0.5)

    def score_tile(j, carry):
        s = jnp.dot(qi, kit_ref[0, j], preferred_element_type=F32)
        s = jnp.maximum(s, 0.0) * wcol
        sc = s[0:tq]
        for h in range(1, H_IDX):
            sc = sc + s[h * tq:(h + 1) * tq]
        sc = jnp.where(sc == 0.0, 0.0, sc)
        bits = pltpu.bitcast(sc, I32)
        key = bits ^ ((bits >> 31) & jnp.int32(0x7FFFFFFF))
        key = jnp.where(lane + j * lt < limit, key, jnp.int32(INT_MIN))
        hi_ref[j] = (key >> 16).astype(I16)
        lo_ref[j] = ((key & 0xFFFF) - 32768).astype(I16)
        return carry

    lax.fori_loop(0, nt, score_tile, 0)

    one16, zero16 = jnp.int16(1), jnp.int16(0)
    min16, max16 = jnp.int16(I16_MIN), jnp.int16(I16_MAX)

    def bcast16(col):
        return jnp.broadcast_to(col.astype(I16), (tq, V7X_LANES))

    def count(pred):
        def body(j, acc):
            hi_t, lo_t = hi_ref[j], lo_ref[j]
            for s in range(n_slab):
                sl = slice(s * V7X_LANES, (s + 1) * V7X_LANES)
                acc = acc + pred(hi_t[:, sl], lo_t[:, sl], j * lt + s * V7X_LANES)
            return acc
        acc = lax.fori_loop(0, nt, body, jnp.zeros((tq, V7X_LANES), I16))
        return jnp.sum(acc.astype(I32), axis=1, keepdims=True)

    def kth_largest(pick, kth):
        def ge_count(cand):
            cb = bcast16(cand)
            return count(lambda h, l, off: jnp.where(pick(h, l) >= cb, one16, zero16))
        ans = jnp.where(ge_count(jnp.zeros((tq, 1), I32)) >= kth, 0, jnp.full((tq, 1), I16_MIN, I32))

        def body(it, ans):
            cand = ans + (jnp.int32(1) << (14 - it))
            return jnp.where(ge_count(cand) >= kth, cand, ans)
        return lax.fori_loop(0, 15, body, ans)

    h_thr = kth_largest(lambda h, l: h, topk)
    hb = bcast16(h_thr)
    c_above = count(lambda h, l, off: jnp.where(h > hb, one16, zero16))
    need2 = topk - c_above
    hb_full = jnp.broadcast_to(h_thr.astype(I16), (tq, lt))

    def bucket_tile(j, carry):
        lo_ref[j] = jnp.where(hi_ref[j] == hb_full, lo_ref[j], min16)
        return carry

    lax.fori_loop(0, nt, bucket_tile, 0)
    l_thr = kth_largest(lambda h, l: l, need2)
    lb = bcast16(l_thr)
    c2_gt = count(lambda h, l, off: jnp.where(l > lb, one16, zero16))
    n_tie = count(lambda h, l, off: jnp.where(h == hb, jnp.where(l == lb, one16, zero16), zero16))
    need_tie = need2 - c2_gt
    all_visible = h_thr == I16_MIN
    tie_rows = (n_tie > need_tie) & jnp.logical_not(all_visible)
    lane16 = lax.broadcasted_iota(I32, (tq, V7X_LANES), 1)

    def tie_cut():
        def body(it, jcut):
            cand = jcut + (jnp.int32(1) << (14 - it))
            cb = bcast16(cand)

            def pred(h, l, off):
                idx = (lane16 + off).astype(I16)
                tie_idx = jnp.where(h == hb, jnp.where(l == lb, idx, max16), max16)
                return jnp.where(tie_idx < cb, one16, zero16)
            return jnp.where(count(pred) <= need_tie, cand, jcut)
        return lax.fori_loop(0, 15, body, jnp.zeros((tq, 1), I32))

    no_cut = jnp.full((tq, 1), I16_MAX, I32)
    any_tie = jnp.max(jnp.where(tie_rows, 1, 0)) > 0
    jcut = lax.cond(any_tie, lambda: jnp.where(tie_rows, tie_cut(), no_cut), lambda: no_cut)
    l_thr = jnp.where(all_visible, I16_MAX, l_thr)
    jcut = jnp.where(all_visible, 0, jcut)

    rows = heads_per_kv * tq
    r_id = lax.broadcasted_iota(I32, (rows, tq), 0) % tq
    c_id = lax.broadcasted_iota(I32, (rows, tq), 1)
    onehot = jnp.where(r_id == c_id, 1.0, 0.0).astype(BF16)
    zpad_l = jnp.zeros((rows, MXU_DEPTH - tq - HD_C), BF16)
    zpad_r = jnp.zeros((MXU_DEPTH - tq - HD_C, lt), BF16)
    lhs = []
    for g in range(KVH_C):
        qh = q_ref[0, g * heads_per_kv:(g + 1) * heads_per_kv].reshape(rows, HD_C) * (HD_C ** -0.5 * LOG2_E)
        lhs.append(jnp.concatenate([onehot, qh.astype(BF16), zpad_l], axis=1))
    hb_t = jnp.broadcast_to(h_thr.astype(I16), (tq, lt))
    lb_t = jnp.broadcast_to(l_thr.astype(I16), (tq, lt))
    lane_t16 = lane.astype(I16)
    v_lane = lax.broadcasted_iota(I32, (lt, KVH_C * HD_C), 1) // HD_C
    keep, drop = jnp.bfloat16(0.0), jnp.bfloat16(MASK_BIAS)

    def att_tile(j, carry):
        hi_t, lo_t = hi_ref[j], lo_ref[j]
        jc_t = jnp.broadcast_to(jnp.clip(jcut - j * lt, I16_MIN, I16_MAX).astype(I16), (tq, lt))
        tie = jnp.where(lo_t == lb_t, jnp.where(lane_t16 < jc_t, keep, drop), drop)
        in_bucket = jnp.where(lo_t > lb_t, keep, tie)
        bias = jnp.where(hi_t > hb_t, keep, jnp.where(hi_t == hb_t, in_bucket, drop))
        v_t = v_ref[0, pl.ds(pl.multiple_of(j * lt, lt), lt), :]
        new = []
        for g in range(KVH_C):
            m_run, acc = carry[g]
            rhs = jnp.concatenate([bias, kt_ref[0, j, g * HD_C:(g + 1) * HD_C, :], zpad_r], axis=0)
            logits = jnp.dot(lhs[g], rhs, preferred_element_type=F32)
            m_new = jnp.maximum(m_run, jnp.max(logits, axis=1, keepdims=True))
            p = jnp.exp2(logits - m_new).astype(BF16)
            alpha = jnp.exp2(m_run - m_new)
            v_aug = jnp.where(v_lane == g, v_t, jnp.bfloat16(1.0))
            acc_new = alpha * acc + jnp.dot(p, v_aug, preferred_element_type=F32)
            new.append((m_new, acc_new))
        return tuple(new)

    init = tuple((jnp.full((rows, 1), -jnp.inf, F32), jnp.zeros((rows, KVH_C * HD_C), F32))
                 for _ in range(KVH_C))
    fin = lax.fori_loop(0, nt, att_tile, init)
    for g in range(KVH_C):
        acc = fin[g][1]
        den = acc[:, (1 - g) * HD_C:(1 - g) * HD_C + 1]
        og = acc / den
        for hh in range(heads_per_kv):
            o_ref[0, g * heads_per_kv + hh] = og[hh * tq:(hh + 1) * tq, g * HD_C:(g + 1) * HD_C]


def _dsa(q_r, qi_r, wi_r, kt4, v_all, kit4, pos0, l_true):
    b, _, t, _ = q_r.shape
    n_tiles, lt = kt4.shape[1], kt4.shape[3]
    lp = n_tiles * lt
    tq = _tile(t, 128)
    topk = min(TOPK_MAX, l_true // 4)
    assert lp <= I16_MAX and KVH_C == 2 and tq + HD_C <= MXU_DEPTH
    kern = functools.partial(_dsa_kernel, tq=tq, topk=topk, pos0=pos0, l_true=l_true, lt=lt)
    return pl.pallas_call(
        kern,
        grid=(b, t // tq),
        in_specs=[
            pl.BlockSpec((1, H_C, tq, HD_C), lambda bi, i: (bi, 0, i, 0)),
            pl.BlockSpec((1, H_IDX, tq, D_IDX), lambda bi, i: (bi, 0, i, 0)),
            pl.BlockSpec((1, H_IDX, tq, 1), lambda bi, i: (bi, 0, i, 0)),
            pl.BlockSpec((1, n_tiles, KVH_C * HD_C, lt), lambda bi, i: (bi, 0, 0, 0)),
            pl.BlockSpec((1, lp, KVH_C * HD_C), lambda bi, i: (bi, 0, 0)),
            pl.BlockSpec((1, n_tiles, D_IDX, lt), lambda bi, i: (bi, 0, 0, 0)),
        ],
        out_specs=pl.BlockSpec((1, H_C, tq, HD_C), lambda bi, i: (bi, 0, i, 0)),
        out_shape=jax.ShapeDtypeStruct((b, H_C, t, HD_C), F32),
        scratch_shapes=[pltpu.VMEM((n_tiles, tq, lt), I16), pltpu.VMEM((n_tiles, tq, lt), I16)],
        compiler_params=_params(("arbitrary", "arbitrary")),
        name="dsa_attn",
    )(q_r, qi_r, wi_r, kt4, v_all, kit4)


def _outproj_kernel(x_ref, mod_ref, oa_ref, hb_ref, oc_ref, w_ref, g_ref, b_ref, o_ref):
    bb, tm, d = x_ref.shape
    flat = lambda ref: ref[...].reshape(bb * tm, ref.shape[2])
    mix = (_dot(flat(oa_ref), w_ref[0:W_A, :]) + _dot(flat(hb_ref), w_ref[W_A:W_A + W_B, :])
           + _dot(flat(oc_ref), w_ref[W_A + W_B:, :]))
    y = ALPHA * x_ref[...] + (1.0 + mod_ref[:, 2:3, :]) * mix.reshape(bb, tm, d)
    o_ref[...] = _layer_norm(y, g_ref[...], b_ref[...])


def _outproj(x, mod, oa, hb, oc, w_out_bf, ln_g, ln_b):
    b, t, d = x.shape
    bb, tm = _row_blocks(b, t, 512)
    row = lambda bi, i: (bi, i, 0)
    mix_w = W_A + W_B + W_C
    return pl.pallas_call(
        _outproj_kernel,
        grid=(b // bb, t // tm),
        in_specs=[
            pl.BlockSpec((bb, tm, d), row),
            pl.BlockSpec((bb, 6, d), lambda bi, i: (bi, 0, 0)),
            pl.BlockSpec((bb, tm, W_A), row), pl.BlockSpec((bb, tm, W_B), row), pl.BlockSpec((bb, tm, W_C), row),
            pl.BlockSpec((mix_w, d), lambda bi, i: (0, 0)),
            pl.BlockSpec((1, d), lambda bi, i: (0, 0)), pl.BlockSpec((1, d), lambda bi, i: (0, 0)),
        ],
        out_specs=pl.BlockSpec((bb, tm, d), row),
        out_shape=jax.ShapeDtypeStruct((b, t, d), F32),
        compiler_params=_params(("arbitrary", "arbitrary")),
        name="out_proj",
    )(x, mod, oa, hb, oc, w_out_bf, ln_g, ln_b)


def _route_t(s_t, sb_t):
    per = N_EXPERTS // N_GROUPS
    gsc = []
    for gi in range(N_GROUPS):
        a, b, c, d = sb_t[gi * per:(gi + 1) * per]
        m1, n1 = jnp.maximum(a, b), jnp.minimum(a, b)
        m2, n2 = jnp.maximum(c, d), jnp.minimum(c, d)
        gsc.append(jnp.maximum(m1, m2) + jnp.maximum(jnp.minimum(m1, m2), jnp.maximum(n1, n2)))
    best, g_sel = gsc[0], jnp.zeros_like(gsc[0], dtype=I32)
    for gi in range(1, N_GROUPS):
        better = gsc[gi] > best
        best = jnp.where(better, gsc[gi], best)
        g_sel = jnp.where(better, gi, g_sel)
    cand = [jnp.where(g_sel == (e // per), sb_t[e], -jnp.inf) for e in range(N_EXPERTS)]
    v1, e1 = cand[0], jnp.zeros_like(g_sel)
    for e in range(1, N_EXPERTS):
        better = cand[e] > v1
        v1 = jnp.where(better, cand[e], v1)
        e1 = jnp.where(better, e, e1)
    v2, e2 = jnp.full_like(v1, -jnp.inf), jnp.full_like(e1, -1)
    for e in range(N_EXPERTS):
        better = (cand[e] > v2) & (e1 != e)
        v2 = jnp.where(better, cand[e], v2)
        e2 = jnp.where(better, e, e2)
    s1 = jnp.zeros_like(v1)
    s2 = jnp.zeros_like(v1)
    for e in range(N_EXPERTS):
        s1 = jnp.where(e1 == e, s_t[e], s1)
        s2 = jnp.where(e2 == e, s_t[e], s2)
    tot = s1 + s2
    return [jnp.where(e1 == e, s1 / tot, jnp.where(e2 == e, s2 / tot, 0.0)) for e in range(N_EXPERTS)]


def _moe_kernel(x_ref, mod_ref, wr_ref, br_ref, wg_ref, wu_ref, wd_ref, g_ref, b_ref, o_ref,
                u_ref, gate_ref, acc_ref):
    e = pl.program_id(2)
    bb, tm, d = x_ref.shape
    rows = bb * tm

    @pl.when(e == 0)
    def _():
        u = (x_ref[...] * (1.0 + mod_ref[:, 4:5, :]) + mod_ref[:, 3:4, :]).reshape(rows, d)
        u_ref[...] = u.astype(BF16)
        s = _sigmoid(_dot_exact(u, wr_ref[...]))
        sb = s + br_ref[...]
        s_t, sb_t = s.T, sb.T
        gates = _route_t([s_t[k:k + 1, :] for k in range(N_EXPERTS)],
                         [sb_t[k:k + 1, :] for k in range(N_EXPERTS)])
        row_id = lax.broadcasted_iota(I32, (N_EXPERTS, rows), 0)
        g16 = jnp.zeros((N_EXPERTS, rows), F32)
        for k in range(N_EXPERTS):
            g16 = jnp.where(row_id == k, gates[k], g16)
        g_t = jnp.concatenate([g16, jnp.zeros((V7X_LANES - N_EXPERTS, rows), F32)], axis=0)
        gate_ref[...] = g_t.T
        acc_ref[...] = jnp.zeros_like(acc_ref)

    u = u_ref[...]
    h = _silu(jnp.dot(u, wg_ref[0], preferred_element_type=F32)) * jnp.dot(u, wu_ref[0], preferred_element_type=F32)
    lane = lax.broadcasted_iota(I32, (rows, V7X_LANES), 1)
    gate = jnp.sum(jnp.where(lane == e, gate_ref[...], 0.0), axis=1, keepdims=True)
    acc_ref[...] += gate * jnp.dot(h.astype(BF16), wd_ref[0], preferred_element_type=F32)

    @pl.when(e == N_EXPERTS - 1)
    def _():
        y = ALPHA * x_ref[...] + (1.0 + mod_ref[:, 5:6, :]) * acc_ref[...].reshape(bb, tm, d)
        o_ref[...] = _layer_norm(y, g_ref[...], b_ref[...])


def _moe(x, mod, wr_p, br_p, wg_bf, wu_bf, wd_bf, ln_g, ln_b):
    b, t, d = x.shape
    bb, tm = _row_blocks(b, t, 1024)
    rows = bb * tm
    row = lambda bi, i, e: (bi, i, 0)
    const2 = lambda bi, i, e: (0, 0)
    return pl.pallas_call(
        _moe_kernel,
        grid=(b // bb, t // tm, N_EXPERTS),
        in_specs=[
            pl.BlockSpec((bb, tm, d), row),
            pl.BlockSpec((bb, 6, d), lambda bi, i, e: (bi, 0, 0)),
            pl.BlockSpec((d, V7X_LANES), const2), pl.BlockSpec((1, V7X_LANES), const2),
            pl.BlockSpec((1, d, D_FF_E), lambda bi, i, e: (e, 0, 0)),
            pl.BlockSpec((1, d, D_FF_E), lambda bi, i, e: (e, 0, 0)),
            pl.BlockSpec((1, D_FF_E, d), lambda bi, i, e: (e, 0, 0)),
            pl.BlockSpec((1, d), const2), pl.BlockSpec((1, d), const2),
        ],
        out_specs=pl.BlockSpec((bb, tm, d), row),
        out_shape=jax.ShapeDtypeStruct((b, t, d), F32),
        scratch_shapes=[pltpu.VMEM((rows, d), BF16), pltpu.VMEM((rows, V7X_LANES), F32), pltpu.VMEM((rows, d), F32)],
        compiler_params=_params(("arbitrary", "arbitrary", "arbitrary")),
        name="moe_ffn",
    )(x, mod, wr_p, br_p, wg_bf, wu_bf, wd_bf, ln_g, ln_b)


def _proj_columns():
    sizes = [H_A * DK_A, H_A * DK_A, W_A, W_A, GATE_RANK, 2 * W_B, W_B, W_B, H_B, H_B,
             W_C, KVH_C * HD_C, KVH_C * HD_C, H_IDX * D_IDX, D_IDX, H_IDX]
    offs = np.concatenate([[0], np.cumsum(sizes)])
    p = int(offs[-1])
    seg = lambda k: np.arange(offs[k], offs[k + 1])
    pad = lambda n: np.full((n,), p)
    qa, ka, va, ga, gr, qk, vb, ob, ib, fb, qc, kc, vc, qi, ki, wi = [seg(k) for k in range(16)]
    cols = np.concatenate([
        qa, ka, va, ga, gr, pad(128 - GATE_RANK),
        qk, vb, ob, ib, fb, pad(128 - 2 * H_B),
        qc, kc, vc, qi, ki, wi, pad(128 - D_IDX - H_IDX)])
    assert cols.shape[0] == P_TOTAL
    return cols, p


def _rope_tables(pos):
    def pattern(hd):
        half = hd // 2
        inv = jnp.power(ROPE_THETA, -jnp.arange(half, dtype=F32) / half)
        ang = pos.astype(F32)[:, None] * inv[None, :]
        cos = jnp.tile(jnp.concatenate([jnp.cos(ang), jnp.cos(ang)], -1), (1, V7X_LANES // hd))
        sin = jnp.tile(jnp.concatenate([-jnp.sin(ang), jnp.sin(ang)], -1), (1, V7X_LANES // hd))
        return cos, sin
    c64, s64 = pattern(HD_C)
    c32, s32 = pattern(D_IDX)
    keep = (jnp.arange(V7X_LANES) < D_IDX)[None, :]
    cm, sm = jnp.where(keep, c32, 1.0), jnp.where(keep, s32, 0.0)
    return (jnp.concatenate([c64, s64], -1), jnp.concatenate([c32, s32], -1), jnp.concatenate([cm, sm], -1))


def _pad_keys(a, lp):
    return jnp.pad(a, ((0, 0), (0, lp - a.shape[1]), (0, 0)))


def _layer(x, mod, lw, shared, tabs, pos0, past, first):
    b, t, d = x.shape
    consts = shared["consts"]
    res = _inproj(x, mod, shared["ln_in_g"], shared["ln_in_b"], lw["w_p"], lw["bias_p"], tabs, first)
    if first:
        x, pa, pb, pc = res
    else:
        pa, pb, pc = res

    if past is None:
        s0_t = jnp.zeros((b, H_A, DV_A, DK_A), F32)
    else:
        s0_t = jnp.swapaxes(past[3], -1, -2)
    oa, sfin_t = _gla(pa, s0_t, lw["wup_p"], lw["bup"], lw["gla_g"], consts)
    s_new = jnp.swapaxes(sfin_t, -1, -2)

    if past is None:
        conv0 = jnp.zeros((b, 8, 2 * W_B), F32)
        c0_st = jnp.zeros((b, W_B, MST_W), F32)
        m0 = jnp.zeros((b, 8, V7X_LANES), F32)
    else:
        conv0 = jnp.pad(past[7], ((0, 0), (8 - (CONV_W - 1), 0), (0, 0)))
        eye = jnp.eye(H_B, dtype=F32)
        c_bd = jnp.einsum('bhkv,hg->bhkgv', past[4], eye).reshape(b, W_B, W_B)
        n_bd = jnp.einsum('bhk,hg->bhkg', past[5], eye).reshape(b, W_B, H_B)
        c0_st = jnp.concatenate([c_bd, n_bd, jnp.zeros((b, W_B, V7X_LANES - H_B), F32)], -1)
        m0 = jnp.pad(jnp.broadcast_to(past[6][:, :, None], (b, H_B, V7X_LANES)), ((0, 0), (0, 8 - H_B), (0, 0)))
    hb, convo, cfin, mfin = _mlstm(pb, conv0, lw["cw_p"], lw["cb"], c0_st, m0, lw["mlstm_g"], consts)
    conv_new = convo[:, 8 - (CONV_W - 1):, :]
    cf = cfin[:, :, :W_B].reshape(b, H_B, DH_B, H_B, DH_B)
    c_new = jnp.stack([cf[:, h, :, h, :] for h in range(H_B)], axis=1)
    nf = cfin[:, :, W_B:W_B + H_B].reshape(b, H_B, DH_B, H_B)
    n_new = jnp.stack([nf[:, h, :, h] for h in range(H_B)], axis=1)
    m_new = mfin[:, :H_B, 0]

    k_new = pc[:, :, 512:640]
    v_new = pc[:, :, 640:768]
    ki_new = pc[:, :, 1024:1024 + D_IDX]
    if past is None:
        k_all, v_all, ki_all = k_new, v_new, ki_new
    else:
        k_all = jnp.concatenate([past[0].reshape(b, -1, KVH_C * HD_C), k_new], axis=1)
        v_all = jnp.concatenate([past[1].reshape(b, -1, KVH_C * HD_C), v_new], axis=1)
        ki_all = jnp.concatenate([past[2], ki_new], axis=1)
    l_true = k_all.shape[1]
    n_tiles = -(-l_true // KEY_TILE_MAX)
    lt = -(-l_true // (n_tiles * V7X_LANES)) * V7X_LANES
    lp = n_tiles * lt
    kt4 = _pad_keys(k_all, lp).astype(BF16).reshape(b, n_tiles, lt, -1).swapaxes(-1, -2)
    kit4 = _pad_keys(ki_all, lp).astype(BF16).reshape(b, n_tiles, lt, -1).swapaxes(-1, -2)
    v_bf = _pad_keys(v_all, lp).astype(BF16)
    q_r = pc[:, :, 0:512].reshape(b, t, H_C, HD_C).swapaxes(1, 2)
    qi_r = pc[:, :, 768:1024].reshape(b, t, H_IDX, D_IDX).swapaxes(1, 2)
    wi_r = pc[:, :, 1024 + D_IDX:1024 + D_IDX + H_IDX].swapaxes(1, 2)[..., None]
    oc_r = _dsa(q_r, qi_r, wi_r, kt4, v_bf, kit4, pos0, l_true)
    oc = oc_r.swapaxes(1, 2).reshape(b, t, W_C)

    x1 = _outproj(x, mod, oa, hb, oc, lw["w_out"], lw["ln1_g"], lw["ln1_b"])
    x2 = _moe(x1, mod, shared["wr_p"], shared["br_p"], lw["wg"], lw["wu"], lw["wd"], lw["ln2_g"], lw["ln2_b"])
    state = (k_new.reshape(b, t, KVH_C, HD_C), v_new.reshape(b, t, KVH_C, HD_C), ki_new,
             s_new, c_new, n_new, m_new, conv_new)
    return x2, state


def kernel(x_prompt, x_sample, c_prompt, c_sample, cache_k, cache_v, cache_kidx, state_gla, state_mlstm_C,
           state_mlstm_n, state_mlstm_m, state_mlstm_conv, ln_in_g, ln_in_b, w_ada, b_ada, w_in, b_in, gla_w_up,
           gla_b_up, gla_norm_g, mlstm_conv_w, mlstm_conv_b, mlstm_norm_g, w_out, ln1_g, ln1_b, w_router,
           b_router, w_gate, w_up, w_down, ln2_g, ln2_b):
    depth = w_in.shape[0]
    d = x_prompt.shape[-1]
    bp, tp, _ = x_prompt.shape
    bs, ts, _ = x_sample.shape
    past_len = cache_k.shape[2]

    cols, p = _proj_columns()
    tril = jnp.tril(jnp.ones((CHUNK, CHUNK), F32))
    ones_bd = (jnp.arange(W_A)[:, None] // DV_A == jnp.arange(W_A)[None, :] // DV_A).astype(BF16)
    shared = dict(
        consts=(tril, ones_bd),
        ln_in_g=ln_in_g.reshape(1, d), ln_in_b=ln_in_b.reshape(1, d),
        wr_p=jnp.pad(w_router, ((0, 0), (0, V7X_LANES - N_EXPERTS))),
        br_p=jnp.pad(b_router, (0, V7X_LANES - N_EXPERTS)).reshape(1, V7X_LANES),
    )
    layers = []
    for l in range(depth):
        w_ext = jnp.concatenate([w_in[l], jnp.zeros((d, 1), F32)], axis=1)
        b_ext = jnp.concatenate([b_in[l], jnp.zeros((1,), F32)])
        layers.append(dict(
            w_p=w_ext[:, cols].astype(BF16), bias_p=b_ext[cols].reshape(1, P_TOTAL),
            wup_p=jnp.pad(gla_w_up[l], ((0, V7X_LANES - GATE_RANK), (0, 0))).astype(BF16),
            bup=gla_b_up[l].reshape(1, -1), gla_g=gla_norm_g[l].reshape(1, -1),
            cw_p=jnp.pad(mlstm_conv_w[l], ((0, 8 - CONV_W), (0, 0))), cb=mlstm_conv_b[l].reshape(1, -1),
            mlstm_g=mlstm_norm_g[l].reshape(1, -1),
            w_out=w_out[l].astype(BF16), ln1_g=ln1_g[l].reshape(1, d), ln1_b=ln1_b[l].reshape(1, d),
            wg=w_gate[l].astype(BF16), wu=w_up[l].astype(BF16), wd=w_down[l].astype(BF16),
            ln2_g=ln2_g[l].reshape(1, d), ln2_b=ln2_b[l].reshape(1, d),
        ))

    mod_all = _ada(jnp.concatenate([c_prompt, c_sample], axis=0), w_ada, b_ada)
    mod_all = mod_all.reshape(depth, bp + bs, 6, d)
    tabs_p = _rope_tables(jnp.arange(tp))
    tabs_s = _rope_tables(past_len + jnp.arange(ts))

    xp, xs = x_prompt, x_sample
    p_states, s_states = [], []
    for l in range(depth):
        xp, st_p = _layer(xp, mod_all[l, :bp], layers[l], shared, tabs_p, 0, None, l == 0)
        p_states.append(st_p)
        past = (cache_k[l], cache_v[l], cache_kidx[l], state_gla[l], state_mlstm_C[l], state_mlstm_n[l],
                state_mlstm_m[l], state_mlstm_conv[l])
        xs, st_s = _layer(xs, mod_all[l, bp:], layers[l], shared, tabs_s, past_len, past, l == 0)
        s_states.append(st_s)

    stk = lambda states, k: jnp.stack([st[k] for st in states], axis=0)
    return ((xp, xs) + tuple(stk(p_states, k) for k in range(8)) + tuple(stk(s_states, k) for k in range(8)))
```

```python
import functools

import numpy as np
import jax
import jax.numpy as jnp
from jax import lax
from jax.experimental import pallas as pl
from jax.experimental.pallas import tpu as pltpu

F32 = jnp.float32
BF16 = jnp.bfloat16
I32 = jnp.int32
HIGHEST = lax.Precision.HIGHEST

CHUNK = 64
H_A, DK_A, DV_A = 4, 32, 64
W_A = H_A * DV_A
GATE_RANK = 16
GLA_TAU = 16.0
H_B, DH_B = 4, 64
W_B = H_B * DH_B
CONV_W = 4
H_C, KVH_C, HD_C = 8, 2, 64
W_C = H_C * HD_C
H_IDX, D_IDX = 8, 32
TOPK_MAX = 256
N_EXPERTS, N_GROUPS = 16, 4
D_FF_E = 256
ROPE_THETA = 10000.0
LN_EPS = 1e-5
DEPTH = 2
ALPHA = (2 * DEPTH) ** 0.25

V7X_LANES = 128
V7X_VMEM_BYTES = 64 * 1024 * 1024
VMEM_LIMIT = (V7X_VMEM_BYTES * 3) // 4

PA_W = 896
PB_W = 1152
PC_W = 1152
P_TOTAL = PA_W + PB_W + PC_W

KEY_TILE_MAX = 2048
INT_MIN = -2 ** 31
I16 = jnp.int16
I16_MIN, I16_MAX = -2 ** 15, 2 ** 15 - 1
LOG2_E = 1.4426950408889634
MASK_BIAS = -1e30


def _tile(n, pref):
    t = min(n, pref)
    while n % t:
        t //= 2
    return t


def _params(sem):
    return pltpu.CompilerParams(dimension_semantics=sem, vmem_limit_bytes=VMEM_LIMIT)


def _dot(a, b):
    return jnp.dot(a.astype(BF16), b.astype(BF16), preferred_element_type=F32)


def _dot_nt(a, b):
    return lax.dot_general(a.astype(BF16), b.astype(BF16), (((1,), (1,)), ((), ())), preferred_element_type=F32)


def _dot_tn(a, b):
    return lax.dot_general(a.astype(BF16), b.astype(BF16), (((0,), (0,)), ((), ())), preferred_element_type=F32)


def _dot_exact(a, b):
    return jnp.dot(a, b, preferred_element_type=F32, precision=HIGHEST)


def _log_sigmoid(x):
    return jnp.minimum(x, 0.0) - jnp.log1p(jnp.exp(-jnp.abs(x)))


def _sigmoid(x):
    return 1.0 / (1.0 + jnp.exp(-x))


def _silu(x):
    return x * _sigmoid(x)


def _layer_norm(x, g, b):
    xc = x - jnp.mean(x, axis=-1, keepdims=True)
    var = jnp.mean(xc * xc, axis=-1, keepdims=True)
    return xc * lax.rsqrt(var + LN_EPS) * g + b


def _ada_kernel(c_ref, w_ref, b_ref, o_ref):
    o_ref[0] = _dot(_silu(c_ref[...]), w_ref[0]) + b_ref[0]


def _ada(c_all, w_ada, b_ada):
    depth, d, n6 = w_ada.shape
    nb = c_all.shape[0]
    tn = 1024
    return pl.pallas_call(
        _ada_kernel,
        grid=(depth, n6 // tn),
        in_specs=[
            pl.BlockSpec((nb, d), lambda l, j: (0, 0)),
            pl.BlockSpec((1, d, tn), lambda l, j: (l, 0, j)),
            pl.BlockSpec((1, 1, tn), lambda l, j: (l, 0, j)),
        ],
        out_specs=pl.BlockSpec((1, nb, tn), lambda l, j: (l, 0, j)),
        out_shape=jax.ShapeDtypeStruct((depth, nb, n6), F32),
        compiler_params=_params(("arbitrary", "arbitrary")),
        name="ada_mod",
    )(c_all, w_ada, b_ada.reshape(depth, 1, n6))


def _rot_half(x, half):
    lane = lax.broadcasted_iota(I32, x.shape, 1)
    first = (lane % (2 * half)) < half
    return jnp.where(first, pltpu.roll(x, V7X_LANES - half, 1), pltpu.roll(x, half, 1))


def _row_blocks(b, t, rows):
    tm = _tile(t, rows)
    bb = _tile(b, max(1, rows // tm)) if tm == t else 1
    return bb, tm


PB_MISC = 1024


def _inproj_kernel(x_ref, mod_ref, g_ref, b_ref, w_ref, bias_ref, wgate_ref, t64_ref, t32_ref, tm_ref,
                   *out_refs, apply_ln):
    if apply_ln:
        xln_ref, pa_ref, pb_ref, pc_ref = out_refs
    else:
        pa_ref, pb_ref, pc_ref = out_refs
    bb, tm, d = x_ref.shape
    x = x_ref[...]
    if apply_ln:
        x = _layer_norm(x, g_ref[...], b_ref[...])
        xln_ref[...] = x
    u32 = (x * (1.0 + mod_ref[:, 1:2, :]) + mod_ref[:, 0:1, :]).reshape(bb * tm, d)
    u = u32.astype(BF16)
    pa = jnp.dot(u, w_ref[:, 0:PA_W], preferred_element_type=F32) + bias_ref[:, 0:PA_W]
    pa_ref[...] = pa.reshape(bb, tm, PA_W)
    pb = (jnp.dot(u, w_ref[:, PA_W:PA_W + PB_W], preferred_element_type=F32)
          + bias_ref[:, PA_W:PA_W + PB_W])
    pb_ref[...] = pb.reshape(bb, tm, PB_W)
    gates = _dot_exact(u32, wgate_ref[...]) + bias_ref[:, PA_W + PB_MISC:PA_W + PB_W]
    pb_ref[:, :, PB_MISC:PB_W] = gates.reshape(bb, tm, PB_W - PB_MISC)
    pc = (jnp.dot(u, w_ref[:, PA_W + PB_W:P_TOTAL], preferred_element_type=F32)
          + bias_ref[:, PA_W + PB_W:P_TOTAL])
    c64, s64 = t64_ref[:, 0:128], t64_ref[:, 128:256]
    c32, s32 = t32_ref[:, 0:128], t32_ref[:, 128:256]
    cm, sm = tm_ref[:, 0:128], tm_ref[:, 128:256]

    def put(s, val):
        pc_ref[:, :, s * 128:(s + 1) * 128] = val.reshape(bb, tm, 128)

    for s in range(5):
        xs = pc[:, s * 128:(s + 1) * 128]
        put(s, xs * c64 + _rot_half(xs, HD_C // 2) * s64)
    put(5, pc[:, 640:768])
    for s in range(6, 8):
        xs = pc[:, s * 128:(s + 1) * 128]
        put(s, xs * c32 + _rot_half(xs, D_IDX // 2) * s32)
    xs = pc[:, 1024:1152]
    put(8, xs * cm + _rot_half(xs, D_IDX // 2) * sm)


def _inproj(x, mod, ln_g, ln_b, w_p, bias_p, w_gate32, tabs, apply_ln):
    b, t, d = x.shape
    bb, tm = _row_blocks(b, t, 512)
    tabs = [jnp.tile(tb, (bb, 1)) for tb in tabs]
    row = lambda bi, i: (bi, i, 0)
    out_shape = [jax.ShapeDtypeStruct((b, t, PA_W), F32), jax.ShapeDtypeStruct((b, t, PB_W), F32),
                 jax.ShapeDtypeStruct((b, t, PC_W), F32)]
    out_specs = [pl.BlockSpec((bb, tm, PA_W), row), pl.BlockSpec((bb, tm, PB_W), row),
                 pl.BlockSpec((bb, tm, PC_W), row)]
    if apply_ln:
        out_shape = [jax.ShapeDtypeStruct((b, t, d), F32)] + out_shape
        out_specs = [pl.BlockSpec((bb, tm, d), row)] + out_specs
    tab_spec = pl.BlockSpec((bb * tm, 256), lambda bi, i: (i, 0))
    return pl.pallas_call(
        functools.partial(_inproj_kernel, apply_ln=apply_ln),
        grid=(b // bb, t // tm),
        in_specs=[
            pl.BlockSpec((bb, tm, d), row),
            pl.BlockSpec((bb, 6, d), lambda bi, i: (bi, 0, 0)),
            pl.BlockSpec((1, d), lambda bi, i: (0, 0)),
            pl.BlockSpec((1, d), lambda bi, i: (0, 0)),
            pl.BlockSpec((d, P_TOTAL), lambda bi, i: (0, 0)),
            pl.BlockSpec((1, P_TOTAL), lambda bi, i: (0, 0)),
            pl.BlockSpec((d, PB_W - PB_MISC), lambda bi, i: (0, 0)),
            tab_spec, tab_spec, tab_spec,
        ],
        out_specs=out_specs,
        out_shape=out_shape,
        compiler_params=_params(("arbitrary", "arbitrary")),
        name="in_proj",
    )(x, mod, ln_g, ln_b, w_p, bias_p, w_gate32, *tabs)


def _seg_sum(x, ones_bd):
    hi = x.astype(BF16)
    lo = (x - hi.astype(F32)).astype(BF16)
    return (jnp.dot(hi, ones_bd, preferred_element_type=F32)
            + jnp.dot(lo, ones_bd, preferred_element_type=F32))


def _gla_kernel(pa_ref, s0_ref, wup_ref, bup_ref, gn_ref, tril_ref, ones_ref, oa_ref, sfin_ref, st_ref, *, nc):
    i = pl.program_id(1)

    @pl.when(i == 0)
    def _():
        st_ref[...] = jnp.zeros_like(st_ref)
        for h in range(H_A):
            st_ref[h * DV_A:(h + 1) * DV_A, h * DK_A:(h + 1) * DK_A] = s0_ref[0, h]

    tril = tril_ref[...]
    ones_bd = ones_ref[...]
    r_sk = lax.broadcasted_iota(I32, (H_A * CHUNK, H_A * DK_A), 0) // CHUNK
    c_sk = lax.broadcasted_iota(I32, (H_A * CHUNK, H_A * DK_A), 1) // DK_A
    mask_sk = r_sk == c_sk
    r_sv = lax.broadcasted_iota(I32, (H_A * CHUNK, W_A), 0) // CHUNK
    c_sv = lax.broadcasted_iota(I32, (H_A * CHUNK, W_A), 1) // DV_A
    mask_sv = r_sv == c_sv
    r_st = lax.broadcasted_iota(I32, (W_A, H_A * DK_A), 0) // DV_A
    c_st = lax.broadcasted_iota(I32, (W_A, H_A * DK_A), 1) // DK_A
    mask_st = r_st == c_st
    t_i = lax.broadcasted_iota(I32, (CHUNK, H_A * CHUNK), 0)
    s_i = lax.broadcasted_iota(I32, (CHUNK, H_A * CHUNK), 1) % CHUNK
    causal = t_i >= s_i

    st = st_ref[...]
    for c in range(nc):
        r = slice(c * CHUNK, (c + 1) * CHUNK)
        q = pa_ref[0, r, 0:128] * (DK_A ** -0.5)
        k = pa_ref[0, r, 128:256]
        v = pa_ref[0, r, 256:512]
        g = pa_ref[0, r, 512:768]
        misc = pa_ref[0, r, 768:896]
        pre = _dot(misc, wup_ref[...]) + bup_ref[...]
        log_a = _log_sigmoid(pre) * (1.0 / GLA_TAU)
        bc = _dot_exact(tril, log_a)
        bm = bc[CHUNK // 2 - 1:CHUNK // 2, :]
        bl = bc[CHUNK - 1:CHUNK, :]
        inter = _dot_nt(q * jnp.exp(bc), st)
        qs = q * jnp.exp(bc - bm)
        ks = k * jnp.exp(bm - bc)
        ks_bd = jnp.where(mask_sk, jnp.concatenate([ks] * H_A, axis=0), 0.0)
        att = jnp.where(causal, _dot_nt(qs, ks_bd), 0.0)
        v_bd = jnp.where(mask_sv, jnp.concatenate([v] * H_A, axis=0), 0.0)
        o = inter + _dot(att, v_bd)
        ms = _seg_sum(o * o, ones_bd) * (1.0 / DV_A)
        oa_ref[0, r, :] = o * lax.rsqrt(ms + LN_EPS) * gn_ref[...] * _silu(g)
        upd = _dot_tn(v, k * jnp.exp(bl - bc))
        st = jnp.exp(bl) * st + jnp.where(mask_st, upd, 0.0)
    st_ref[...] = st
    for h in range(H_A):
        sfin_ref[0, h] = st[h * DV_A:(h + 1) * DV_A, h * DK_A:(h + 1) * DK_A]


def _gla(pa, s0_t, wup_p, bup, gnorm, consts):
    b, t, _ = pa.shape
    tb = _tile(t, 512)
    tril, ones_bd = consts
    full = lambda shape: pl.BlockSpec(shape, lambda bi, i: (0,) * len(shape))
    return pl.pallas_call(
        functools.partial(_gla_kernel, nc=tb // CHUNK),
        grid=(b, t // tb),
        in_specs=[
            pl.BlockSpec((1, tb, PA_W), lambda bi, i: (bi, i, 0)),
            pl.BlockSpec((1, H_A, DV_A, DK_A), lambda bi, i: (bi, 0, 0, 0)),
            full((128, 128)), full((1, 128)), full((1, W_A)), full((CHUNK, CHUNK)), full((W_A, W_A)),
        ],
        out_specs=[
            pl.BlockSpec((1, tb, W_A), lambda bi, i: (bi, i, 0)),
            pl.BlockSpec((1, H_A, DV_A, DK_A), lambda bi, i: (bi, 0, 0, 0)),
        ],
        out_shape=[jax.ShapeDtypeStruct((b, t, W_A), F32),
                   jax.ShapeDtypeStruct((b, H_A, DV_A, DK_A), F32)],
        scratch_shapes=[pltpu.VMEM((W_A, H_A * DK_A), F32)],
        compiler_params=_params(("arbitrary", "arbitrary")),
        name="gla_scan",
    )(pa, s0_t, wup_p, bup, gnorm, tril, ones_bd)


MST_W = W_B + V7X_LANES


def _expand_heads(cols, lane_head):
    out = jnp.zeros(lane_head.shape, F32)
    for h, col in enumerate(cols):
        out = jnp.where(lane_head == h, col, out)
    return out


def _mlstm_kernel(pb_ref, conv0_ref, cw_ref, cb_ref, c0_ref, m0_ref, gn_ref, tril_ref, ones_ref,
                  hb_ref, convo_ref, cfin_ref, mfin_ref, xs_ref, cst_ref, m_ref, *, nc):
    i = pl.program_id(1)
    tb = nc * CHUNK

    @pl.when(i == 0)
    def _():
        xs_ref[0:8, :] = conv0_ref[0]
        cst_ref[...] = c0_ref[0]
        m_ref[...] = m0_ref[0]

    xs_ref[8:8 + tb, :] = pb_ref[0, :, 0:512]
    tril = tril_ref[...]
    ones_bd = ones_ref[...]

    lane_h = lax.broadcasted_iota(I32, (CHUNK, W_B), 1) // DH_B
    t_i = lax.broadcasted_iota(I32, (CHUNK, W_B), 0)
    s_i = lax.broadcasted_iota(I32, (CHUNK, W_B), 1) % CHUNK
    causal = t_i >= s_i
    diag = t_i == s_i
    e_r = lax.broadcasted_iota(I32, (V7X_LANES, W_B), 0)
    e_c = lax.broadcasted_iota(I32, (V7X_LANES, W_B), 1) // DH_B
    e_f = (e_r == e_c + H_B).astype(F32)
    e_i = (e_r == e_c).astype(F32)
    bd_r = lax.broadcasted_iota(I32, (W_B, W_B), 0) // DH_B
    bd_c = lax.broadcasted_iota(I32, (W_B, W_B), 1) // DH_B
    mask_bd = bd_r == bd_c
    st_lane = lax.broadcasted_iota(I32, (CHUNK, MST_W), 1)
    st_lane_h = jnp.where(st_lane < W_B, st_lane // DH_B, jnp.where(st_lane < W_B + H_B, st_lane - W_B, -1))
    row_h = lax.broadcasted_iota(I32, (W_B, MST_W), 0) // DH_B
    col = lax.broadcasted_iota(I32, (W_B, MST_W), 1)
    col_h = jnp.where(col < W_B, col // DH_B, jnp.where(col < W_B + H_B, col - W_B, -1))
    mask_state = row_h == col_h
    ones_lane = (lax.broadcasted_iota(I32, (CHUNK, V7X_LANES), 1) < H_B).astype(F32)
    ones_sq = jnp.ones((CHUNK, CHUNK), F32)

    cst = cst_ref[...]
    m_prev = [m_ref[h:h + 1, 0:1] for h in range(H_B)]
    for c in range(nc):
        base = 8 + c * CHUNK
        r = slice(c * CHUNK, (c + 1) * CHUNK)
        conv = cb_ref[...]
        for j in range(CONV_W):
            conv = conv + xs_ref[base - (CONV_W - 1) + j:base - (CONV_W - 1) + j + CHUNK, :] * cw_ref[j:j + 1, :]
        act = _silu(conv)
        q = act[:, 0:W_B]
        k = act[:, W_B:2 * W_B] * (DH_B ** -0.5)
        v = pb_ref[0, r, 512:768]
        og = pb_ref[0, r, 768:1024]
        gates = pb_ref[0, r, 1024:1152]
        bcum = _dot_exact(tril, _log_sigmoid(gates))
        b_exp = _dot_exact(bcum, e_f)
        i_exp = _dot_exact(gates, e_i)
        row_term = _dot_exact(ones_sq, jnp.where(diag, i_exp - b_exp, 0.0))
        dmat = jnp.where(causal, b_exp + row_term, -jnp.inf)
        bcol = [bcum[:, H_B + h:H_B + h + 1] for h in range(H_B)]
        icol = [gates[:, h:h + 1] for h in range(H_B)]
        ginter = [bcol[h] + m_prev[h] for h in range(H_B)]
        mt = [jnp.maximum(ginter[h], jnp.max(jnp.where(lane_h == h, dmat, -jnp.inf), axis=1, keepdims=True))
              for h in range(H_B)]
        mt_exp = _expand_heads(mt, lane_h)
        k_bd = jnp.where(mask_bd, jnp.concatenate([k] * H_B, axis=0), 0.0)
        qk = _dot_nt(q, k_bd) * jnp.exp(dmat - mt_exp)
        v_aug = jnp.concatenate([v, ones_lane], axis=1)
        v_bd = jnp.where(mask_state, jnp.concatenate([v_aug] * H_B, axis=0), 0.0)
        w_inter = _expand_heads([jnp.exp(ginter[h] - mt[h]) for h in range(H_B)], st_lane_h)
        tot = w_inter * _dot(q, cst) + _dot(qk, v_bd)
        num = tot[:, 0:W_B]
        den = _expand_heads([tot[:, W_B + h:W_B + h + 1] for h in range(H_B)], lane_h)
        hh = num / jnp.maximum(jnp.abs(den), jnp.exp(-mt_exp))
        mu = _seg_sum(hh, ones_bd) * (1.0 / DH_B)
        hc = hh - mu
        var = _seg_sum(hc * hc, ones_bd) * (1.0 / DH_B)
        hb_ref[0, r, :] = hc * lax.rsqrt(var + LN_EPS) * gn_ref[...] * _sigmoid(og)
        m_new = [mt[h][CHUNK - 1:CHUNK, :] for h in range(H_B)]
        b_last = [bcol[h][CHUNK - 1:CHUNK, :] for h in range(H_B)]
        w_c = [jnp.exp(b_last[h] + m_prev[h] - m_new[h]) for h in range(H_B)]
        w_s = _expand_heads([jnp.exp(b_last[h] - bcol[h] + icol[h] - m_new[h]) for h in range(H_B)], lane_h)
        upd = _dot_tn(k * w_s, v_aug)
        w_c_rows = jnp.zeros((W_B, MST_W), F32)
        for h in range(H_B):
            w_c_rows = jnp.where(row_h == h, w_c[h], w_c_rows)
        cst = w_c_rows * cst + jnp.where(mask_state, upd, 0.0)
        m_prev = m_new

    cst_ref[...] = cst
    for h in range(H_B):
        m_ref[h:h + 1, :] = jnp.broadcast_to(m_prev[h], (1, V7X_LANES))
    tail = xs_ref[tb:tb + 8, :]
    xs_ref[0:8, :] = tail
    convo_ref[0] = tail
    cfin_ref[0] = cst
    mfin_ref[0] = m_ref[...]


def _mlstm(pb, conv0_p, cw_p, cb, c0_st, m0_p, gnorm, consts):
    b, t, _ = pb.shape
    tb = _tile(t, 512)
    tril, ones_bd = consts
    full = lambda shape: pl.BlockSpec(shape, lambda bi, i: (0,) * len(shape))
    per_b = lambda shape: pl.BlockSpec((1,) + shape, lambda bi, i: (bi,) + (0,) * len(shape))
    return pl.pallas_call(
        functools.partial(_mlstm_kernel, nc=tb // CHUNK),
        grid=(b, t // tb),
        in_specs=[
            pl.BlockSpec((1, tb, PB_W), lambda bi, i: (bi, i, 0)),
            per_b((8, 2 * W_B)), full((8, 2 * W_B)), full((1, 2 * W_B)),
            per_b((W_B, MST_W)), per_b((8, V7X_LANES)), full((1, W_B)), full((CHUNK, CHUNK)), full((W_B, W_B)),
        ],
        out_specs=[
            pl.BlockSpec((1, tb, W_B), lambda bi, i: (bi, i, 0)),
            per_b((8, 2 * W_B)), per_b((W_B, MST_W)), per_b((8, V7X_LANES)),
        ],
        out_shape=[jax.ShapeDtypeStruct((b, t, W_B), F32),
                   jax.ShapeDtypeStruct((b, 8, 2 * W_B), F32),
                   jax.ShapeDtypeStruct((b, W_B, MST_W), F32),
                   jax.ShapeDtypeStruct((b, 8, V7X_LANES), F32)],
        scratch_shapes=[pltpu.VMEM((tb + 8, 2 * W_B), F32), pltpu.VMEM((W_B, MST_W), F32),
                        pltpu.VMEM((8, V7X_LANES), F32)],
        compiler_params=_params(("arbitrary", "arbitrary")),
        name="mlstm_scan",
    )(pb, conv0_p, cw_p, cb, c0_st, m0_p, gnorm, tril, ones_bd)


def _dsa_kernel(q_ref, qi_ref, wi_ref, kt_ref, v_ref, kit_ref, o_ref, hi_ref, lo_ref,
                *, tq, topk, pos0, l_true, lt):
    qb = pl.program_id(1)
    n_slab = lt // V7X_LANES
    heads_per_kv = H_C // KVH_C
    t_row = lax.broadcasted_iota(I32, (tq, 1), 0) + qb * tq + pos0
    limit = jnp.minimum((t_row // CHUNK + 1) * CHUNK, l_true)
    last_limit = jnp.minimum(((qb * tq + tq - 1 + pos0) // CHUNK + 1) * CHUNK, l_true)
    nt = (last_limit + lt - 1) // lt
    lane = lax.broadcasted_iota(I32, (tq, lt), 1)

    qi = qi_ref[0].reshape(H_IDX * tq, D_IDX).astype(BF16)
    wcol = wi_ref[0].reshape(H_IDX * tq, 1) * (H_IDX ** -0.5 * D_IDX ** -0.5)

    def score_tile(j, carry):
        s = jnp.dot(qi, kit_ref[0, j], preferred_element_type=F32)
        s = jnp.maximum(s, 0.0) * wcol
        sc = s[0:tq]
        for h in range(1, H_IDX):
            sc = sc + s[h * tq:(h + 1) * tq]
        sc = jnp.where(sc == 0.0, 0.0, sc)
        bits = pltpu.bitcast(sc, I32)
        key = bits ^ ((bits >> 31) & jnp.int32(0x7FFFFFFF))
        key = jnp.where(lane + j * lt < limit, key, jnp.int32(INT_MIN))
        hi_ref[j] = (key >> 16).astype(I16)
        lo_ref[j] = ((key & 0xFFFF) - 32768).astype(I16)
        return carry

    lax.fori_loop(0, nt, score_tile, 0)

    one16, zero16 = jnp.int16(1), jnp.int16(0)
    min16, max16 = jnp.int16(I16_MIN), jnp.int16(I16_MAX)

    def bcast16(col):
        return jnp.broadcast_to(col.astype(I16), (tq, V7X_LANES))

    def count(pred):
        def body(j, acc):
            hi_t, lo_t = hi_ref[j], lo_ref[j]
            for s in range(n_slab):
                sl = slice(s * V7X_LANES, (s + 1) * V7X_LANES)
                acc = acc + pred(hi_t[:, sl], lo_t[:, sl], j * lt + s * V7X_LANES)
            return acc
        acc = lax.fori_loop(0, nt, body, jnp.zeros((tq, V7X_LANES), I16))
        return jnp.sum(acc.astype(I32), axis=1, keepdims=True)

    def kth_largest(pick, kth):
        def ge_count(cand):
            cb = bcast16(cand)
            return count(lambda h, l, off: jnp.where(pick(h, l) >= cb, one16, zero16))

        def step(cand, state):
            ans, c_ans, c_next = state
            cnt = ge_count(cand)
            ok = cnt >= kth
            return jnp.where(ok, cand, ans), jnp.where(ok, cnt, c_ans), jnp.where(ok, c_next, cnt)

        zero = jnp.zeros((tq, 1), I32)
        state = step(zero, (jnp.full((tq, 1), I16_MIN, I32), zero, zero))
        return lax.fori_loop(0, 15, lambda it, st: step(st[0] + (jnp.int32(1) << (14 - it)), st), state)

    h_thr, _, c_above = kth_largest(lambda h, l: h, topk)
    hb = bcast16(h_thr)
    need2 = topk - c_above
    hb_full = jnp.broadcast_to(h_thr.astype(I16), (tq, lt))

    def bucket_tile(j, carry):
        lo_ref[j] = jnp.where(hi_ref[j] == hb_full, lo_ref[j], min16)
        return carry

    lax.fori_loop(0, nt, bucket_tile, 0)
    l_thr, c2_ge, c2_gt = kth_largest(lambda h, l: l, need2)
    lb = bcast16(l_thr)
    need_tie = need2 - c2_gt
    all_visible = h_thr == I16_MIN
    n_tie = jnp.where(l_thr == I16_MIN, I16_MAX, c2_ge - c2_gt)
    tie_rows = (n_tie > need_tie) & jnp.logical_not(all_visible)
    lane16 = lax.broadcasted_iota(I32, (tq, V7X_LANES), 1)

    def tie_cut():
        def body(it, jcut):
            cand = jcut + (jnp.int32(1) << (14 - it))
            cb = bcast16(cand)

            def pred(h, l, off):
                idx = (lane16 + off).astype(I16)
                tie_idx = jnp.where(h == hb, jnp.where(l == lb, idx, max16), max16)
                return jnp.where(tie_idx < cb, one16, zero16)
            return jnp.where(count(pred) <= need_tie, cand, jcut)
        return lax.fori_loop(0, 15, body, jnp.zeros((tq, 1), I32))

    no_cut = jnp.full((tq, 1), I16_MAX, I32)
    any_tie = jnp.max(jnp.where(tie_rows, 1, 0)) > 0
    jcut = lax.cond(any_tie, lambda: jnp.where(tie_rows, tie_cut(), no_cut), lambda: no_cut)
    l_thr = jnp.where(all_visible, I16_MAX, l_thr)
    jcut = jnp.where(all_visible, 0, jcut)

    rows = heads_per_kv * tq
    lhs = []
    for g in range(KVH_C):
        qh = q_ref[0, g * heads_per_kv:(g + 1) * heads_per_kv].reshape(rows, HD_C) * (HD_C ** -0.5 * LOG2_E)
        lhs.append(qh.astype(BF16))
    hb_t = jnp.broadcast_to(h_thr.astype(I16), (tq, lt))
    lb_t = jnp.broadcast_to(l_thr.astype(I16), (tq, lt))
    lane_t16 = lane.astype(I16)
    v_lane = lax.broadcasted_iota(I32, (lt, KVH_C * HD_C), 1) // HD_C
    keep, drop = jnp.bfloat16(0.0), jnp.bfloat16(MASK_BIAS)

    def att_tile(j, carry):
        hi_t, lo_t = hi_ref[j], lo_ref[j]
        jc_t = jnp.broadcast_to(jnp.clip(jcut - j * lt, I16_MIN, I16_MAX).astype(I16), (tq, lt))
        tie = jnp.where(lo_t == lb_t, jnp.where(lane_t16 < jc_t, keep, drop), drop)
        in_bucket = jnp.where(lo_t > lb_t, keep, tie)
        bias = jnp.where(hi_t > hb_t, keep, jnp.where(hi_t == hb_t, in_bucket, drop))
        bias = bias.astype(F32)[None]
        v_t = v_ref[0, pl.ds(pl.multiple_of(j * lt, lt), lt), :]
        new = []
        for g in range(KVH_C):
            m_run, acc = carry[g]
            logits = jnp.dot(lhs[g], kt_ref[0, j, g * HD_C:(g + 1) * HD_C, :], preferred_element_type=F32)
            logits = (logits.reshape(heads_per_kv, tq, lt) + bias).reshape(rows, lt)
            m_new = jnp.maximum(m_run, jnp.max(logits, axis=1, keepdims=True))
            p = jnp.exp2(logits - m_new).astype(BF16)
            alpha = jnp.exp2(m_run - m_new)
            v_aug = jnp.where(v_lane == g, v_t, jnp.bfloat16(1.0))
            acc_new = alpha * acc + jnp.dot(p, v_aug, preferred_element_type=F32)
            new.append((m_new, acc_new))
        return tuple(new)

    init = tuple((jnp.full((rows, 1), -jnp.inf, F32), jnp.zeros((rows, KVH_C * HD_C), F32))
                 for _ in range(KVH_C))
    fin = lax.fori_loop(0, nt, att_tile, init)
    for g in range(KVH_C):
        acc = fin[g][1]
        den = acc[:, (1 - g) * HD_C:(1 - g) * HD_C + 1]
        og = acc / den
        for hh in range(heads_per_kv):
            o_ref[0, g * heads_per_kv + hh] = og[hh * tq:(hh + 1) * tq, g * HD_C:(g + 1) * HD_C]


def _dsa(q_r, qi_r, wi_r, kt4, v_all, kit4, pos0, l_true):
    b, _, t, _ = q_r.shape
    n_tiles, lt = kt4.shape[1], kt4.shape[3]
    lp = n_tiles * lt
    tq = _tile(t, 128)
    topk = min(TOPK_MAX, l_true // 4)
    assert lp <= I16_MAX and KVH_C == 2
    kern = functools.partial(_dsa_kernel, tq=tq, topk=topk, pos0=pos0, l_true=l_true, lt=lt)
    return pl.pallas_call(
        kern,
        grid=(b, t // tq),
        in_specs=[
            pl.BlockSpec((1, H_C, tq, HD_C), lambda bi, i: (bi, 0, i, 0)),
            pl.BlockSpec((1, H_IDX, tq, D_IDX), lambda bi, i: (bi, 0, i, 0)),
            pl.BlockSpec((1, H_IDX, tq, 1), lambda bi, i: (bi, 0, i, 0)),
            pl.BlockSpec((1, n_tiles, KVH_C * HD_C, lt), lambda bi, i: (bi, 0, 0, 0)),
            pl.BlockSpec((1, lp, KVH_C * HD_C), lambda bi, i: (bi, 0, 0)),
            pl.BlockSpec((1, n_tiles, D_IDX, lt), lambda bi, i: (bi, 0, 0, 0)),
        ],
        out_specs=pl.BlockSpec((1, H_C, tq, HD_C), lambda bi, i: (bi, 0, i, 0)),
        out_shape=jax.ShapeDtypeStruct((b, H_C, t, HD_C), F32),
        scratch_shapes=[pltpu.VMEM((n_tiles, tq, lt), I16), pltpu.VMEM((n_tiles, tq, lt), I16)],
        compiler_params=_params(("arbitrary", "arbitrary")),
        name="dsa_attn",
    )(q_r, qi_r, wi_r, kt4, v_all, kit4)


def _outproj_kernel(x_ref, mod_ref, oa_ref, hb_ref, oc_ref, w_ref, g_ref, b_ref, o_ref):
    bb, tm, d = x_ref.shape
    flat = lambda ref: ref[...].reshape(bb * tm, ref.shape[2])
    mix = (_dot(flat(oa_ref), w_ref[0:W_A, :]) + _dot(flat(hb_ref), w_ref[W_A:W_A + W_B, :])
           + _dot(flat(oc_ref), w_ref[W_A + W_B:, :]))
    y = ALPHA * x_ref[...] + (1.0 + mod_ref[:, 2:3, :]) * mix.reshape(bb, tm, d)
    o_ref[...] = _layer_norm(y, g_ref[...], b_ref[...])


def _outproj(x, mod, oa, hb, oc, w_out_bf, ln_g, ln_b):
    b, t, d = x.shape
    bb, tm = _row_blocks(b, t, 512)
    row = lambda bi, i: (bi, i, 0)
    mix_w = W_A + W_B + W_C
    return pl.pallas_call(
        _outproj_kernel,
        grid=(b // bb, t // tm),
        in_specs=[
            pl.BlockSpec((bb, tm, d), row),
            pl.BlockSpec((bb, 6, d), lambda bi, i: (bi, 0, 0)),
            pl.BlockSpec((bb, tm, W_A), row), pl.BlockSpec((bb, tm, W_B), row), pl.BlockSpec((bb, tm, W_C), row),
            pl.BlockSpec((mix_w, d), lambda bi, i: (0, 0)),
            pl.BlockSpec((1, d), lambda bi, i: (0, 0)), pl.BlockSpec((1, d), lambda bi, i: (0, 0)),
        ],
        out_specs=pl.BlockSpec((bb, tm, d), row),
        out_shape=jax.ShapeDtypeStruct((b, t, d), F32),
        compiler_params=_params(("arbitrary", "arbitrary")),
        name="out_proj",
    )(x, mod, oa, hb, oc, w_out_bf, ln_g, ln_b)


def _route_t(s_t, sb_t):
    per = N_EXPERTS // N_GROUPS
    gsc = []
    for gi in range(N_GROUPS):
        a, b, c, d = sb_t[gi * per:(gi + 1) * per]
        m1, n1 = jnp.maximum(a, b), jnp.minimum(a, b)
        m2, n2 = jnp.maximum(c, d), jnp.minimum(c, d)
        gsc.append(jnp.maximum(m1, m2) + jnp.maximum(jnp.minimum(m1, m2), jnp.maximum(n1, n2)))
    best, g_sel = gsc[0], jnp.zeros_like(gsc[0], dtype=I32)
    for gi in range(1, N_GROUPS):
        better = gsc[gi] > best
        best = jnp.where(better, gsc[gi], best)
        g_sel = jnp.where(better, gi, g_sel)
    cand = [jnp.where(g_sel == (e // per), sb_t[e], -jnp.inf) for e in range(N_EXPERTS)]
    v1, e1 = cand[0], jnp.zeros_like(g_sel)
    for e in range(1, N_EXPERTS):
        better = cand[e] > v1
        v1 = jnp.where(better, cand[e], v1)
        e1 = jnp.where(better, e, e1)
    v2, e2 = jnp.full_like(v1, -jnp.inf), jnp.full_like(e1, -1)
    for e in range(N_EXPERTS):
        better = (cand[e] > v2) & (e1 != e)
        v2 = jnp.where(better, cand[e], v2)
        e2 = jnp.where(better, e, e2)
    s1 = jnp.zeros_like(v1)
    s2 = jnp.zeros_like(v1)
    for e in range(N_EXPERTS):
        s1 = jnp.where(e1 == e, s_t[e], s1)
        s2 = jnp.where(e2 == e, s_t[e], s2)
    tot = s1 + s2
    return [jnp.where(e1 == e, s1 / tot, jnp.where(e2 == e, s2 / tot, 0.0)) for e in range(N_EXPERTS)]


def _moe_kernel(x_ref, mod_ref, wr_ref, br_ref, wg_ref, wu_ref, wd_ref, g_ref, b_ref, o_ref,
                u_ref, gate_ref, acc_ref):
    e = pl.program_id(2)
    bb, tm, d = x_ref.shape
    rows = bb * tm

    @pl.when(e == 0)
    def _():
        u = (x_ref[...] * (1.0 + mod_ref[:, 4:5, :]) + mod_ref[:, 3:4, :]).reshape(rows, d)
        u_ref[...] = u.astype(BF16)
        s = _sigmoid(_dot_exact(u, wr_ref[...]))
        sb = s + br_ref[...]
        s_t, sb_t = s.T, sb.T
        gates = _route_t([s_t[k:k + 1, :] for k in range(N_EXPERTS)],
                         [sb_t[k:k + 1, :] for k in range(N_EXPERTS)])
        row_id = lax.broadcasted_iota(I32, (N_EXPERTS, rows), 0)
        g16 = jnp.zeros((N_EXPERTS, rows), F32)
        for k in range(N_EXPERTS):
            g16 = jnp.where(row_id == k, gates[k], g16)
        g_t = jnp.concatenate([g16, jnp.zeros((V7X_LANES - N_EXPERTS, rows), F32)], axis=0)
        gate_ref[...] = g_t.T
        acc_ref[...] = jnp.zeros_like(acc_ref)

    u = u_ref[...]
    h = _silu(jnp.dot(u, wg_ref[0], preferred_element_type=F32)) * jnp.dot(u, wu_ref[0], preferred_element_type=F32)
    lane = lax.broadcasted_iota(I32, (rows, V7X_LANES), 1)
    gate = jnp.sum(jnp.where(lane == e, gate_ref[...], 0.0), axis=1, keepdims=True)
    acc_ref[...] += gate * jnp.dot(h.astype(BF16), wd_ref[0], preferred_element_type=F32)

    @pl.when(e == N_EXPERTS - 1)
    def _():
        y = ALPHA * x_ref[...] + (1.0 + mod_ref[:, 5:6, :]) * acc_ref[...].reshape(bb, tm, d)
        o_ref[...] = _layer_norm(y, g_ref[...], b_ref[...])


def _moe(x, mod, wr_p, br_p, wg_bf, wu_bf, wd_bf, ln_g, ln_b):
    b, t, d = x.shape
    bb, tm = _row_blocks(b, t, 1024)
    rows = bb * tm
    row = lambda bi, i, e: (bi, i, 0)
    const2 = lambda bi, i, e: (0, 0)
    return pl.pallas_call(
        _moe_kernel,
        grid=(b // bb, t // tm, N_EXPERTS),
        in_specs=[
            pl.BlockSpec((bb, tm, d), row),
            pl.BlockSpec((bb, 6, d), lambda bi, i, e: (bi, 0, 0)),
            pl.BlockSpec((d, V7X_LANES), const2), pl.BlockSpec((1, V7X_LANES), const2),
            pl.BlockSpec((1, d, D_FF_E), lambda bi, i, e: (e, 0, 0)),
            pl.BlockSpec((1, d, D_FF_E), lambda bi, i, e: (e, 0, 0)),
            pl.BlockSpec((1, D_FF_E, d), lambda bi, i, e: (e, 0, 0)),
            pl.BlockSpec((1, d), const2), pl.BlockSpec((1, d), const2),
        ],
        out_specs=pl.BlockSpec((bb, tm, d), row),
        out_shape=jax.ShapeDtypeStruct((b, t, d), F32),
        scratch_shapes=[pltpu.VMEM((rows, d), BF16), pltpu.VMEM((rows, V7X_LANES), F32), pltpu.VMEM((rows, d), F32)],
        compiler_params=_params(("arbitrary", "arbitrary", "arbitrary")),
        name="moe_ffn",
    )(x, mod, wr_p, br_p, wg_bf, wu_bf, wd_bf, ln_g, ln_b)


def _proj_columns():
    sizes = [H_A * DK_A, H_A * DK_A, W_A, W_A, GATE_RANK, 2 * W_B, W_B, W_B, H_B, H_B,
             W_C, KVH_C * HD_C, KVH_C * HD_C, H_IDX * D_IDX, D_IDX, H_IDX]
    offs = np.concatenate([[0], np.cumsum(sizes)])
    p = int(offs[-1])
    seg = lambda k: np.arange(offs[k], offs[k + 1])
    pad = lambda n: np.full((n,), p)
    qa, ka, va, ga, gr, qk, vb, ob, ib, fb, qc, kc, vc, qi, ki, wi = [seg(k) for k in range(16)]
    cols = np.concatenate([
        qa, ka, va, ga, gr, pad(128 - GATE_RANK),
        qk, vb, ob, ib, fb, pad(128 - 2 * H_B),
        qc, kc, vc, qi, ki, wi, pad(128 - D_IDX - H_IDX)])
    assert cols.shape[0] == P_TOTAL
    return cols, p


def _rope_tables(pos):
    def pattern(hd):
        half = hd // 2
        inv = jnp.power(ROPE_THETA, -jnp.arange(half, dtype=F32) / half)
        ang = pos.astype(F32)[:, None] * inv[None, :]
        cos = jnp.tile(jnp.concatenate([jnp.cos(ang), jnp.cos(ang)], -1), (1, V7X_LANES // hd))
        sin = jnp.tile(jnp.concatenate([-jnp.sin(ang), jnp.sin(ang)], -1), (1, V7X_LANES // hd))
        return cos, sin
    c64, s64 = pattern(HD_C)
    c32, s32 = pattern(D_IDX)
    keep = (jnp.arange(V7X_LANES) < D_IDX)[None, :]
    cm, sm = jnp.where(keep, c32, 1.0), jnp.where(keep, s32, 0.0)
    return (jnp.concatenate([c64, s64], -1), jnp.concatenate([c32, s32], -1), jnp.concatenate([cm, sm], -1))


def _pad_keys(a, lp):
    return jnp.pad(a, ((0, 0), (0, lp - a.shape[1]), (0, 0)))


def _layer(x, mod, lw, shared, tabs, pos0, past, first):
    b, t, d = x.shape
    consts = shared["consts"]
    res = _inproj(x, mod, shared["ln_in_g"], shared["ln_in_b"], lw["w_p"], lw["bias_p"], lw["w_gate32"], tabs, first)
    if first:
        x, pa, pb, pc = res
    else:
        pa, pb, pc = res

    if past is None:
        s0_t = jnp.zeros((b, H_A, DV_A, DK_A), F32)
    else:
        s0_t = jnp.swapaxes(past[3], -1, -2)
    oa, sfin_t = _gla(pa, s0_t, lw["wup_p"], lw["bup"], lw["gla_g"], consts)
    s_new = jnp.swapaxes(sfin_t, -1, -2)

    if past is None:
        conv0 = jnp.zeros((b, 8, 2 * W_B), F32)
        c0_st = jnp.zeros((b, W_B, MST_W), F32)
        m0 = jnp.zeros((b, 8, V7X_LANES), F32)
    else:
        conv0 = jnp.pad(past[7], ((0, 0), (8 - (CONV_W - 1), 0), (0, 0)))
        eye = jnp.eye(H_B, dtype=F32)
        c_bd = jnp.einsum('bhkv,hg->bhkgv', past[4], eye).reshape(b, W_B, W_B)
        n_bd = jnp.einsum('bhk,hg->bhkg', past[5], eye).reshape(b, W_B, H_B)
        c0_st = jnp.concatenate([c_bd, n_bd, jnp.zeros((b, W_B, V7X_LANES - H_B), F32)], -1)
        m0 = jnp.pad(jnp.broadcast_to(past[6][:, :, None], (b, H_B, V7X_LANES)), ((0, 0), (0, 8 - H_B), (0, 0)))
    hb, convo, cfin, mfin = _mlstm(pb, conv0, lw["cw_p"], lw["cb"], c0_st, m0, lw["mlstm_g"], consts)
    conv_new = convo[:, 8 - (CONV_W - 1):, :]
    cf = cfin[:, :, :W_B].reshape(b, H_B, DH_B, H_B, DH_B)
    c_new = jnp.stack([cf[:, h, :, h, :] for h in range(H_B)], axis=1)
    nf = cfin[:, :, W_B:W_B + H_B].reshape(b, H_B, DH_B, H_B)
    n_new = jnp.stack([nf[:, h, :, h] for h in range(H_B)], axis=1)
    m_new = mfin[:, :H_B, 0]

    k_new = pc[:, :, 512:640]
    v_new = pc[:, :, 640:768]
    ki_new = pc[:, :, 1024:1024 + D_IDX]
    if past is None:
        k_all, v_all, ki_all = k_new, v_new, ki_new
    else:
        k_all = jnp.concatenate([past[0].reshape(b, -1, KVH_C * HD_C), k_new], axis=1)
        v_all = jnp.concatenate([past[1].reshape(b, -1, KVH_C * HD_C), v_new], axis=1)
        ki_all = jnp.concatenate([past[2], ki_new], axis=1)
    l_true = k_all.shape[1]
    n_tiles = -(-l_true // KEY_TILE_MAX)
    lt = -(-l_true // (n_tiles * V7X_LANES)) * V7X_LANES
    lp = n_tiles * lt
    kt4 = _pad_keys(k_all, lp).astype(BF16).reshape(b, n_tiles, lt, -1).swapaxes(-1, -2)
    kit4 = _pad_keys(ki_all, lp).astype(BF16).reshape(b, n_tiles, lt, -1).swapaxes(-1, -2)
    v_bf = _pad_keys(v_all, lp).astype(BF16)
    q_r = pc[:, :, 0:512].reshape(b, t, H_C, HD_C).swapaxes(1, 2)
    qi_r = pc[:, :, 768:1024].reshape(b, t, H_IDX, D_IDX).swapaxes(1, 2)
    wi_r = pc[:, :, 1024 + D_IDX:1024 + D_IDX + H_IDX].swapaxes(1, 2)[..., None]
    oc_r = _dsa(q_r, qi_r, wi_r, kt4, v_bf, kit4, pos0, l_true)
    oc = oc_r.swapaxes(1, 2).reshape(b, t, W_C)

    x1 = _outproj(x, mod, oa, hb, oc, lw["w_out"], lw["ln1_g"], lw["ln1_b"])
    x2 = _moe(x1, mod, shared["wr_p"], shared["br_p"], lw["wg"], lw["wu"], lw["wd"], lw["ln2_g"], lw["ln2_b"])
    state = (k_new.reshape(b, t, KVH_C, HD_C), v_new.reshape(b, t, KVH_C, HD_C), ki_new,
             s_new, c_new, n_new, m_new, conv_new)
    return x2, state


def kernel(x_prompt, x_sample, c_prompt, c_sample, cache_k, cache_v, cache_kidx, state_gla, state_mlstm_C,
           state_mlstm_n, state_mlstm_m, state_mlstm_conv, ln_in_g, ln_in_b, w_ada, b_ada, w_in, b_in, gla_w_up,
           gla_b_up, gla_norm_g, mlstm_conv_w, mlstm_conv_b, mlstm_norm_g, w_out, ln1_g, ln1_b, w_router,
           b_router, w_gate, w_up, w_down, ln2_g, ln2_b):
    depth = w_in.shape[0]
    d = x_prompt.shape[-1]
    bp, tp, _ = x_prompt.shape
    bs, ts, _ = x_sample.shape
    past_len = cache_k.shape[2]

    cols, p = _proj_columns()
    tril = jnp.tril(jnp.ones((CHUNK, CHUNK), F32))
    ones_bd = (jnp.arange(W_A)[:, None] // DV_A == jnp.arange(W_A)[None, :] // DV_A).astype(BF16)
    shared = dict(
        consts=(tril, ones_bd),
        ln_in_g=ln_in_g.reshape(1, d), ln_in_b=ln_in_b.reshape(1, d),
        wr_p=jnp.pad(w_router, ((0, 0), (0, V7X_LANES - N_EXPERTS))),
        br_p=jnp.pad(b_router, (0, V7X_LANES - N_EXPERTS)).reshape(1, V7X_LANES),
    )
    layers = []
    for l in range(depth):
        w_ext = jnp.concatenate([w_in[l], jnp.zeros((d, 1), F32)], axis=1)
        b_ext = jnp.concatenate([b_in[l], jnp.zeros((1,), F32)])
        layers.append(dict(
            w_p=w_ext[:, cols].astype(BF16), bias_p=b_ext[cols].reshape(1, P_TOTAL),
            w_gate32=w_ext[:, cols[PA_W + PB_MISC:PA_W + PB_W]],
            wup_p=jnp.pad(gla_w_up[l], ((0, V7X_LANES - GATE_RANK), (0, 0))).astype(BF16),
            bup=gla_b_up[l].reshape(1, -1), gla_g=gla_norm_g[l].reshape(1, -1),
            cw_p=jnp.pad(mlstm_conv_w[l], ((0, 8 - CONV_W), (0, 0))), cb=mlstm_conv_b[l].reshape(1, -1),
            mlstm_g=mlstm_norm_g[l].reshape(1, -1),
            w_out=w_out[l].astype(BF16), ln1_g=ln1_g[l].reshape(1, d), ln1_b=ln1_b[l].reshape(1, d),
            wg=w_gate[l].astype(BF16), wu=w_up[l].astype(BF16), wd=w_down[l].astype(BF16),
            ln2_g=ln2_g[l].reshape(1, d), ln2_b=ln2_b[l].reshape(1, d),
        ))

    mod_all = _ada(jnp.concatenate([c_prompt, c_sample], axis=0), w_ada, b_ada)
    mod_all = mod_all.reshape(depth, bp + bs, 6, d)
    tabs_p = _rope_tables(jnp.arange(tp))
    tabs_s = _rope_tables(past_len + jnp.arange(ts))

    xp, xs = x_prompt, x_sample
    p_states, s_states = [], []
    for l in range(depth):
        xp, st_p = _layer(xp, mod_all[l, :bp], layers[l], shared, tabs_p, 0, None, l == 0)
        p_states.append(st_p)
        past = (cache_k[l], cache_v[l], cache_kidx[l], state_gla[l], state_mlstm_C[l], state_mlstm_n[l],
                state_mlstm_m[l], state_mlstm_conv[l])
        xs, st_s = _layer(xs, mod_all[l, bp:], layers[l], shared, tabs_s, past_len, past, l == 0)
        s_states.append(st_s)

    stk = lambda states, k: jnp.stack([st[k] for st in states], axis=0)
    return ((xp, xs) + tuple(stk(p_states, k) for k in range(8)) + tuple(stk(s_states, k) for k in range(8)))
```

```python
import functools

import numpy as np
import jax
import jax.numpy as jnp
from jax import lax
from jax.experimental import pallas as pl
from jax.experimental.pallas import tpu as pltpu

F32 = jnp.float32
BF16 = jnp.bfloat16
I32 = jnp.int32
HIGHEST = lax.Precision.HIGHEST

CHUNK = 64
H_A, DK_A, DV_A = 4, 32, 64
W_A = H_A * DV_A
GATE_RANK = 16
GLA_TAU = 16.0
H_B, DH_B = 4, 64
W_B = H_B * DH_B
CONV_W = 4
H_C, KVH_C, HD_C = 8, 2, 64
W_C = H_C * HD_C
H_IDX, D_IDX = 8, 32
TOPK_MAX = 256
N_EXPERTS, N_GROUPS = 16, 4
D_FF_E = 256
ROPE_THETA = 10000.0
LN_EPS = 1e-5
DEPTH = 2
ALPHA = (2 * DEPTH) ** 0.25

V7X_LANES = 128
V7X_VMEM_BYTES = 64 * 1024 * 1024
VMEM_LIMIT = (V7X_VMEM_BYTES * 3) // 4

PA_W = 896
PB_W = 1152
PC_W = 1152
P_TOTAL = PA_W + PB_W + PC_W

KEY_TILE_MAX = 2048
INT_MIN = -2 ** 31
I16 = jnp.int16
I16_MIN, I16_MAX = -2 ** 15, 2 ** 15 - 1
LOG2_E = 1.4426950408889634
MASK_BIAS = -1e30


def _tile(n, pref):
    t = min(n, pref)
    while n % t:
        t //= 2
    return t


def _params(sem):
    return pltpu.CompilerParams(dimension_semantics=sem, vmem_limit_bytes=VMEM_LIMIT)


def _dot(a, b):
    return jnp.dot(a.astype(BF16), b.astype(BF16), preferred_element_type=F32)


def _dot_nt(a, b):
    return lax.dot_general(a.astype(BF16), b.astype(BF16), (((1,), (1,)), ((), ())), preferred_element_type=F32)


def _dot_tn(a, b):
    return lax.dot_general(a.astype(BF16), b.astype(BF16), (((0,), (0,)), ((), ())), preferred_element_type=F32)


def _dot_exact(a, b):
    return jnp.dot(a, b, preferred_element_type=F32, precision=HIGHEST)


def _log_sigmoid(x):
    return jnp.minimum(x, 0.0) - jnp.log1p(jnp.exp(-jnp.abs(x)))


def _sigmoid(x):
    return 1.0 / (1.0 + jnp.exp(-x))


def _silu(x):
    return x * _sigmoid(x)


def _layer_norm(x, g, b):
    xc = x - jnp.mean(x, axis=-1, keepdims=True)
    var = jnp.mean(xc * xc, axis=-1, keepdims=True)
    return xc * lax.rsqrt(var + LN_EPS) * g + b


def _ada_kernel(c_ref, w_ref, b_ref, o_ref):
    o_ref[0] = _dot(_silu(c_ref[...]), w_ref[0]) + b_ref[0]


def _ada(c_all, w_ada, b_ada):
    depth, d, n6 = w_ada.shape
    nb = c_all.shape[0]
    tn = 1024
    return pl.pallas_call(
        _ada_kernel,
        grid=(depth, n6 // tn),
        in_specs=[
            pl.BlockSpec((nb, d), lambda l, j: (0, 0)),
            pl.BlockSpec((1, d, tn), lambda l, j: (l, 0, j)),
            pl.BlockSpec((1, 1, tn), lambda l, j: (l, 0, j)),
        ],
        out_specs=pl.BlockSpec((1, nb, tn), lambda l, j: (l, 0, j)),
        out_shape=jax.ShapeDtypeStruct((depth, nb, n6), F32),
        compiler_params=_params(("arbitrary", "arbitrary")),
        name="ada_mod",
    )(c_all, w_ada, b_ada.reshape(depth, 1, n6))


def _rot_half(x, half):
    lane = lax.broadcasted_iota(I32, x.shape, 1)
    first = (lane % (2 * half)) < half
    return jnp.where(first, pltpu.roll(x, V7X_LANES - half, 1), pltpu.roll(x, half, 1))


def _row_blocks(b, t, rows):
    tm = _tile(t, rows)
    bb = _tile(b, max(1, rows // tm)) if tm == t else 1
    return bb, tm


PB_MISC = 1024


def _inproj_kernel(x_ref, mod_ref, g_ref, b_ref, w_ref, bias_ref, wgate_ref, t64_ref, t32_ref, tm_ref,
                   *out_refs, apply_ln, emit_keys):
    out_refs = list(out_refs)
    xln_ref = out_refs.pop(0) if apply_ln else None
    pa_ref, pb_ref, pc_ref = out_refs[:3]
    bb, tm, d = x_ref.shape
    x = x_ref[...]
    if apply_ln:
        x = _layer_norm(x, g_ref[...], b_ref[...])
        xln_ref[...] = x
    u32 = (x * (1.0 + mod_ref[:, 1:2, :]) + mod_ref[:, 0:1, :]).reshape(bb * tm, d)
    u = u32.astype(BF16)
    pa = jnp.dot(u, w_ref[:, 0:PA_W], preferred_element_type=F32) + bias_ref[:, 0:PA_W]
    pa_ref[...] = pa.reshape(bb, tm, PA_W)
    pb = (jnp.dot(u, w_ref[:, PA_W:PA_W + PB_W], preferred_element_type=F32)
          + bias_ref[:, PA_W:PA_W + PB_W])
    pb_ref[...] = pb.reshape(bb, tm, PB_W)
    gates = _dot_exact(u32, wgate_ref[...]) + bias_ref[:, PA_W + PB_MISC:PA_W + PB_W]
    pb_ref[:, :, PB_MISC:PB_W] = gates.reshape(bb, tm, PB_W - PB_MISC)
    pc = (jnp.dot(u, w_ref[:, PA_W + PB_W:P_TOTAL], preferred_element_type=F32)
          + bias_ref[:, PA_W + PB_W:P_TOTAL])
    c64, s64 = t64_ref[:, 0:128], t64_ref[:, 128:256]
    c32, s32 = t32_ref[:, 0:128], t32_ref[:, 128:256]
    cm, sm = tm_ref[:, 0:128], tm_ref[:, 128:256]

    def put(s, val):
        pc_ref[:, :, s * 128:(s + 1) * 128] = val.reshape(bb, tm, 128)

    rot64 = []
    for s in range(5):
        xs = pc[:, s * 128:(s + 1) * 128]
        rot64.append(xs * c64 + _rot_half(xs, HD_C // 2) * s64)
        put(s, rot64[s])
    put(5, pc[:, 640:768])
    for s in range(6, 8):
        xs = pc[:, s * 128:(s + 1) * 128]
        put(s, xs * c32 + _rot_half(xs, D_IDX // 2) * s32)
    xs = pc[:, 1024:1152]
    misc = xs * cm + _rot_half(xs, D_IDX // 2) * sm
    put(8, misc)
    if emit_keys:
        kt_ref, kit_ref, vbf_ref = out_refs[3:]
        kt_ref[0, 0] = rot64[4].T.astype(BF16)
        kit_ref[0, 0] = misc.T[0:D_IDX].astype(BF16)
        vbf_ref[0] = pc[:, 640:768].astype(BF16)


def _inproj(x, mod, ln_g, ln_b, w_p, bias_p, w_gate32, tabs, apply_ln, key_tile):
    b, t, d = x.shape
    bb, tm = _row_blocks(b, t, 512)
    emit_keys = key_tile is not None
    tabs = [jnp.tile(tb, (bb, 1)) for tb in tabs]
    row = lambda bi, i: (bi, i, 0)
    out_shape = [jax.ShapeDtypeStruct((b, t, PA_W), F32), jax.ShapeDtypeStruct((b, t, PB_W), F32),
                 jax.ShapeDtypeStruct((b, t, PC_W), F32)]
    out_specs = [pl.BlockSpec((bb, tm, PA_W), row), pl.BlockSpec((bb, tm, PB_W), row),
                 pl.BlockSpec((bb, tm, PC_W), row)]
    if apply_ln:
        out_shape = [jax.ShapeDtypeStruct((b, t, d), F32)] + out_shape
        out_specs = [pl.BlockSpec((bb, tm, d), row)] + out_specs
    if emit_keys:
        assert bb == 1 and key_tile % tm == 0 and t % key_tile == 0
        per = key_tile // tm
        tile_idx = lambda bi, i: (bi, i // per, 0, i % per)
        out_shape += [jax.ShapeDtypeStruct((b, t // key_tile, KVH_C * HD_C, key_tile), BF16),
                      jax.ShapeDtypeStruct((b, t // key_tile, D_IDX, key_tile), BF16),
                      jax.ShapeDtypeStruct((b, t, KVH_C * HD_C), BF16)]
        out_specs += [pl.BlockSpec((1, 1, KVH_C * HD_C, tm), tile_idx),
                      pl.BlockSpec((1, 1, D_IDX, tm), tile_idx),
                      pl.BlockSpec((1, tm, KVH_C * HD_C), row)]
    tab_spec = pl.BlockSpec((bb * tm, 256), lambda bi, i: (i, 0))
    return pl.pallas_call(
        functools.partial(_inproj_kernel, apply_ln=apply_ln, emit_keys=emit_keys),
        grid=(b // bb, t // tm),
        in_specs=[
            pl.BlockSpec((bb, tm, d), row),
            pl.BlockSpec((bb, 6, d), lambda bi, i: (bi, 0, 0)),
            pl.BlockSpec((1, d), lambda bi, i: (0, 0)),
            pl.BlockSpec((1, d), lambda bi, i: (0, 0)),
            pl.BlockSpec((d, P_TOTAL), lambda bi, i: (0, 0)),
            pl.BlockSpec((1, P_TOTAL), lambda bi, i: (0, 0)),
            pl.BlockSpec((d, PB_W - PB_MISC), lambda bi, i: (0, 0)),
            tab_spec, tab_spec, tab_spec,
        ],
        out_specs=out_specs,
        out_shape=out_shape,
        compiler_params=_params(("arbitrary", "arbitrary")),
        name="in_proj",
    )(x, mod, ln_g, ln_b, w_p, bias_p, w_gate32, *tabs)


def _seg_sum(x, ones_bd):
    hi = x.astype(BF16)
    lo = (x - hi.astype(F32)).astype(BF16)
    return (jnp.dot(hi, ones_bd, preferred_element_type=F32)
            + jnp.dot(lo, ones_bd, preferred_element_type=F32))


def _gla_kernel(pa_ref, s0_ref, wup_ref, bup_ref, gn_ref, tril_ref, ones_ref, oa_ref, sfin_ref, st_ref, *, nc):
    i = pl.program_id(1)

    @pl.when(i == 0)
    def _():
        st_ref[...] = jnp.zeros_like(st_ref)
        for h in range(H_A):
            st_ref[h * DV_A:(h + 1) * DV_A, h * DK_A:(h + 1) * DK_A] = s0_ref[0, h]

    tril = tril_ref[...]
    ones_bd = ones_ref[...]
    r_sk = lax.broadcasted_iota(I32, (H_A * CHUNK, H_A * DK_A), 0) // CHUNK
    c_sk = lax.broadcasted_iota(I32, (H_A * CHUNK, H_A * DK_A), 1) // DK_A
    mask_sk = r_sk == c_sk
    r_sv = lax.broadcasted_iota(I32, (H_A * CHUNK, W_A), 0) // CHUNK
    c_sv = lax.broadcasted_iota(I32, (H_A * CHUNK, W_A), 1) // DV_A
    mask_sv = r_sv == c_sv
    r_st = lax.broadcasted_iota(I32, (W_A, H_A * DK_A), 0) // DV_A
    c_st = lax.broadcasted_iota(I32, (W_A, H_A * DK_A), 1) // DK_A
    mask_st = r_st == c_st
    t_i = lax.broadcasted_iota(I32, (CHUNK, H_A * CHUNK), 0)
    s_i = lax.broadcasted_iota(I32, (CHUNK, H_A * CHUNK), 1) % CHUNK
    causal = t_i >= s_i

    st = st_ref[...]
    for c in range(nc):
        r = slice(c * CHUNK, (c + 1) * CHUNK)
        q = pa_ref[0, r, 0:128] * (DK_A ** -0.5)
        k = pa_ref[0, r, 128:256]
        v = pa_ref[0, r, 256:512]
        g = pa_ref[0, r, 512:768]
        misc = pa_ref[0, r, 768:896]
        pre = _dot(misc, wup_ref[...]) + bup_ref[...]
        log_a = _log_sigmoid(pre) * (1.0 / GLA_TAU)
        bc = _dot_exact(tril, log_a)
        bm = bc[CHUNK // 2 - 1:CHUNK // 2, :]
        bl = bc[CHUNK - 1:CHUNK, :]
        inter = _dot_nt(q * jnp.exp(bc), st)
        qs = q * jnp.exp(bc - bm)
        ks = k * jnp.exp(bm - bc)
        ks_bd = jnp.where(mask_sk, jnp.concatenate([ks] * H_A, axis=0), 0.0)
        att = jnp.where(causal, _dot_nt(qs, ks_bd), 0.0)
        v_bd = jnp.where(mask_sv, jnp.concatenate([v] * H_A, axis=0), 0.0)
        o = inter + _dot(att, v_bd)
        ms = _seg_sum(o * o, ones_bd) * (1.0 / DV_A)
        oa_ref[0, r, :] = o * lax.rsqrt(ms + LN_EPS) * gn_ref[...] * _silu(g)
        upd = _dot_tn(v, k * jnp.exp(bl - bc))
        st = jnp.exp(bl) * st + jnp.where(mask_st, upd, 0.0)
    st_ref[...] = st
    for h in range(H_A):
        sfin_ref[0, h] = st[h * DV_A:(h + 1) * DV_A, h * DK_A:(h + 1) * DK_A]


def _gla(pa, s0_t, wup_p, bup, gnorm, consts):
    b, t, _ = pa.shape
    tb = _tile(t, 512)
    tril, ones_bd = consts
    full = lambda shape: pl.BlockSpec(shape, lambda bi, i: (0,) * len(shape))
    return pl.pallas_call(
        functools.partial(_gla_kernel, nc=tb // CHUNK),
        grid=(b, t // tb),
        in_specs=[
            pl.BlockSpec((1, tb, PA_W), lambda bi, i: (bi, i, 0)),
            pl.BlockSpec((1, H_A, DV_A, DK_A), lambda bi, i: (bi, 0, 0, 0)),
            full((128, 128)), full((1, 128)), full((1, W_A)), full((CHUNK, CHUNK)), full((W_A, W_A)),
        ],
        out_specs=[
            pl.BlockSpec((1, tb, W_A), lambda bi, i: (bi, i, 0)),
            pl.BlockSpec((1, H_A, DV_A, DK_A), lambda bi, i: (bi, 0, 0, 0)),
        ],
        out_shape=[jax.ShapeDtypeStruct((b, t, W_A), F32),
                   jax.ShapeDtypeStruct((b, H_A, DV_A, DK_A), F32)],
        scratch_shapes=[pltpu.VMEM((W_A, H_A * DK_A), F32)],
        compiler_params=_params(("arbitrary", "arbitrary")),
        name="gla_scan",
    )(pa, s0_t, wup_p, bup, gnorm, tril, ones_bd)


MST_W = W_B + V7X_LANES


def _expand_heads(cols, lane_head):
    out = jnp.zeros(lane_head.shape, F32)
    for h, col in enumerate(cols):
        out = jnp.where(lane_head == h, col, out)
    return out


def _mlstm_kernel(pb_ref, conv0_ref, cw_ref, cb_ref, c0_ref, m0_ref, gn_ref, tril_ref, ones_ref,
                  hb_ref, convo_ref, cfin_ref, mfin_ref, xs_ref, cst_ref, m_ref, *, nc):
    i = pl.program_id(1)
    tb = nc * CHUNK

    @pl.when(i == 0)
    def _():
        xs_ref[0:8, :] = conv0_ref[0]
        cst_ref[...] = c0_ref[0]
        m_ref[...] = m0_ref[0]

    xs_ref[8:8 + tb, :] = pb_ref[0, :, 0:512]
    tril = tril_ref[...]
    ones_bd = ones_ref[...]

    lane_h = lax.broadcasted_iota(I32, (CHUNK, W_B), 1) // DH_B
    t_i = lax.broadcasted_iota(I32, (CHUNK, W_B), 0)
    s_i = lax.broadcasted_iota(I32, (CHUNK, W_B), 1) % CHUNK
    causal = t_i >= s_i
    diag = t_i == s_i
    e_r = lax.broadcasted_iota(I32, (V7X_LANES, W_B), 0)
    e_c = lax.broadcasted_iota(I32, (V7X_LANES, W_B), 1) // DH_B
    e_f = (e_r == e_c + H_B).astype(F32)
    e_i = (e_r == e_c).astype(F32)
    bd_r = lax.broadcasted_iota(I32, (W_B, W_B), 0) // DH_B
    bd_c = lax.broadcasted_iota(I32, (W_B, W_B), 1) // DH_B
    mask_bd = bd_r == bd_c
    st_lane = lax.broadcasted_iota(I32, (CHUNK, MST_W), 1)
    st_lane_h = jnp.where(st_lane < W_B, st_lane // DH_B, jnp.where(st_lane < W_B + H_B, st_lane - W_B, -1))
    row_h = lax.broadcasted_iota(I32, (W_B, MST_W), 0) // DH_B
    col = lax.broadcasted_iota(I32, (W_B, MST_W), 1)
    col_h = jnp.where(col < W_B, col // DH_B, jnp.where(col < W_B + H_B, col - W_B, -1))
    mask_state = row_h == col_h
    ones_lane = (lax.broadcasted_iota(I32, (CHUNK, V7X_LANES), 1) < H_B).astype(F32)
    ones_sq = jnp.ones((CHUNK, CHUNK), F32)

    cst = cst_ref[...]
    m_prev = [m_ref[h:h + 1, 0:1] for h in range(H_B)]
    for c in range(nc):
        base = 8 + c * CHUNK
        r = slice(c * CHUNK, (c + 1) * CHUNK)
        conv = cb_ref[...]
        for j in range(CONV_W):
            conv = conv + xs_ref[base - (CONV_W - 1) + j:base - (CONV_W - 1) + j + CHUNK, :] * cw_ref[j:j + 1, :]
        act = _silu(conv)
        q = act[:, 0:W_B]
        k = act[:, W_B:2 * W_B] * (DH_B ** -0.5)
        v = pb_ref[0, r, 512:768]
        og = pb_ref[0, r, 768:1024]
        gates = pb_ref[0, r, 1024:1152]
        bcum = _dot_exact(tril, _log_sigmoid(gates))
        b_exp = _dot_exact(bcum, e_f)
        i_exp = _dot_exact(gates, e_i)
        row_term = _dot_exact(ones_sq, jnp.where(diag, i_exp - b_exp, 0.0))
        dmat = jnp.where(causal, b_exp + row_term, -jnp.inf)
        bcol = [bcum[:, H_B + h:H_B + h + 1] for h in range(H_B)]
        icol = [gates[:, h:h + 1] for h in range(H_B)]
        ginter = [bcol[h] + m_prev[h] for h in range(H_B)]
        mt = [jnp.maximum(ginter[h], jnp.max(jnp.where(lane_h == h, dmat, -jnp.inf), axis=1, keepdims=True))
              for h in range(H_B)]
        mt_exp = _expand_heads(mt, lane_h)
        k_bd = jnp.where(mask_bd, jnp.concatenate([k] * H_B, axis=0), 0.0)
        qk = _dot_nt(q, k_bd) * jnp.exp(dmat - mt_exp)
        v_aug = jnp.concatenate([v, ones_lane], axis=1)
        v_bd = jnp.where(mask_state, jnp.concatenate([v_aug] * H_B, axis=0), 0.0)
        w_inter = _expand_heads([jnp.exp(ginter[h] - mt[h]) for h in range(H_B)], st_lane_h)
        tot = w_inter * _dot(q, cst) + _dot(qk, v_bd)
        num = tot[:, 0:W_B]
        den = _expand_heads([tot[:, W_B + h:W_B + h + 1] for h in range(H_B)], lane_h)
        hh = num / jnp.maximum(jnp.abs(den), jnp.exp(-mt_exp))
        mu = _seg_sum(hh, ones_bd) * (1.0 / DH_B)
        hc = hh - mu
        var = _seg_sum(hc * hc, ones_bd) * (1.0 / DH_B)
        hb_ref[0, r, :] = hc * lax.rsqrt(var + LN_EPS) * gn_ref[...] * _sigmoid(og)
        m_new = [mt[h][CHUNK - 1:CHUNK, :] for h in range(H_B)]
        b_last = [bcol[h][CHUNK - 1:CHUNK, :] for h in range(H_B)]
        w_c = [jnp.exp(b_last[h] + m_prev[h] - m_new[h]) for h in range(H_B)]
        w_s = _expand_heads([jnp.exp(b_last[h] - bcol[h] + icol[h] - m_new[h]) for h in range(H_B)], lane_h)
        upd = _dot_tn(k * w_s, v_aug)
        w_c_rows = jnp.zeros((W_B, MST_W), F32)
        for h in range(H_B):
            w_c_rows = jnp.where(row_h == h, w_c[h], w_c_rows)
        cst = w_c_rows * cst + jnp.where(mask_state, upd, 0.0)
        m_prev = m_new

    cst_ref[...] = cst
    for h in range(H_B):
        m_ref[h:h + 1, :] = jnp.broadcast_to(m_prev[h], (1, V7X_LANES))
    tail = xs_ref[tb:tb + 8, :]
    xs_ref[0:8, :] = tail
    convo_ref[0] = tail
    cfin_ref[0] = cst
    mfin_ref[0] = m_ref[...]


def _mlstm(pb, conv0_p, cw_p, cb, c0_st, m0_p, gnorm, consts):
    b, t, _ = pb.shape
    tb = _tile(t, 512)
    tril, ones_bd = consts
    full = lambda shape: pl.BlockSpec(shape, lambda bi, i: (0,) * len(shape))
    per_b = lambda shape: pl.BlockSpec((1,) + shape, lambda bi, i: (bi,) + (0,) * len(shape))
    return pl.pallas_call(
        functools.partial(_mlstm_kernel, nc=tb // CHUNK),
        grid=(b, t // tb),
        in_specs=[
            pl.BlockSpec((1, tb, PB_W), lambda bi, i: (bi, i, 0)),
            per_b((8, 2 * W_B)), full((8, 2 * W_B)), full((1, 2 * W_B)),
            per_b((W_B, MST_W)), per_b((8, V7X_LANES)), full((1, W_B)), full((CHUNK, CHUNK)), full((W_B, W_B)),
        ],
        out_specs=[
            pl.BlockSpec((1, tb, W_B), lambda bi, i: (bi, i, 0)),
            per_b((8, 2 * W_B)), per_b((W_B, MST_W)), per_b((8, V7X_LANES)),
        ],
        out_shape=[jax.ShapeDtypeStruct((b, t, W_B), F32),
                   jax.ShapeDtypeStruct((b, 8, 2 * W_B), F32),
                   jax.ShapeDtypeStruct((b, W_B, MST_W), F32),
                   jax.ShapeDtypeStruct((b, 8, V7X_LANES), F32)],
        scratch_shapes=[pltpu.VMEM((tb + 8, 2 * W_B), F32), pltpu.VMEM((W_B, MST_W), F32),
                        pltpu.VMEM((8, V7X_LANES), F32)],
        compiler_params=_params(("arbitrary", "arbitrary")),
        name="mlstm_scan",
    )(pb, conv0_p, cw_p, cb, c0_st, m0_p, gnorm, tril, ones_bd)


def _dsa_kernel(q_ref, qi_ref, misc_ref, kt_ref, v_ref, kit_ref, o_ref, hi_ref, lo_ref,
                *, tq, topk, pos0, l_true, lt):
    qb = pl.program_id(1)
    n_slab = lt // V7X_LANES
    heads_per_kv = H_C // KVH_C
    t_row = lax.broadcasted_iota(I32, (tq, 1), 0) + qb * tq + pos0
    limit = jnp.minimum((t_row // CHUNK + 1) * CHUNK, l_true)
    last_limit = jnp.minimum(((qb * tq + tq - 1 + pos0) // CHUNK + 1) * CHUNK, l_true)
    nt = (last_limit + lt - 1) // lt
    lane = lax.broadcasted_iota(I32, (tq, lt), 1)

    def head_rows(x, first, count, width):
        return jnp.concatenate([x[:, h * width:(h + 1) * width] for h in range(first, first + count)], axis=0)

    qi = head_rows(qi_ref[0], 0, H_IDX, D_IDX).astype(BF16)
    wcol = head_rows(misc_ref[0][:, D_IDX:D_IDX + H_IDX], 0, H_IDX, 1) * (H_IDX ** -0.5 * D_IDX ** -0.5)

    def score_tile(j, carry):
        s = jnp.dot(qi, kit_ref[0, j], preferred_element_type=F32)
        s = jnp.maximum(s, 0.0) * wcol
        sc = s[0:tq]
        for h in range(1, H_IDX):
            sc = sc + s[h * tq:(h + 1) * tq]
        sc = jnp.where(sc == 0.0, 0.0, sc)
        bits = pltpu.bitcast(sc, I32)
        key = bits ^ ((bits >> 31) & jnp.int32(0x7FFFFFFF))
        key = jnp.where(lane + j * lt < limit, key, jnp.int32(INT_MIN))
        hi_ref[j] = (key >> 16).astype(I16)
        lo_ref[j] = ((key & 0xFFFF) - 32768).astype(I16)
        return carry

    lax.fori_loop(0, nt, score_tile, 0)

    one16, zero16 = jnp.int16(1), jnp.int16(0)
    min16, max16 = jnp.int16(I16_MIN), jnp.int16(I16_MAX)

    def bcast16(col):
        return jnp.broadcast_to(col.astype(I16), (tq, V7X_LANES))

    def count(pred):
        def body(j, acc):
            hi_t, lo_t = hi_ref[j], lo_ref[j]
            for s in range(n_slab):
                sl = slice(s * V7X_LANES, (s + 1) * V7X_LANES)
                acc = acc + pred(hi_t[:, sl], lo_t[:, sl], j * lt + s * V7X_LANES)
            return acc
        acc = lax.fori_loop(0, nt, body, jnp.zeros((tq, V7X_LANES), I16))
        return jnp.sum(acc.astype(I32), axis=1, keepdims=True)

    def kth_largest(pick, kth):
        def ge_count(cand):
            cb = bcast16(cand)
            return count(lambda h, l, off: jnp.where(pick(h, l) >= cb, one16, zero16))

        def step(cand, state):
            ans, c_ans, c_next = state
            cnt = ge_count(cand)
            ok = cnt >= kth
            return jnp.where(ok, cand, ans), jnp.where(ok, cnt, c_ans), jnp.where(ok, c_next, cnt)

        zero = jnp.zeros((tq, 1), I32)
        state = step(zero, (jnp.full((tq, 1), I16_MIN, I32), zero, zero))
        return lax.fori_loop(0, 15, lambda it, st: step(st[0] + (jnp.int32(1) << (14 - it)), st), state)

    h_thr, _, c_above = kth_largest(lambda h, l: h, topk)
    hb = bcast16(h_thr)
    need2 = topk - c_above
    hb_full = jnp.broadcast_to(h_thr.astype(I16), (tq, lt))

    def bucket_tile(j, carry):
        lo_ref[j] = jnp.where(hi_ref[j] == hb_full, lo_ref[j], min16)
        return carry

    lax.fori_loop(0, nt, bucket_tile, 0)
    l_thr, c2_ge, c2_gt = kth_largest(lambda h, l: l, need2)
    lb = bcast16(l_thr)
    need_tie = need2 - c2_gt
    all_visible = h_thr == I16_MIN
    n_tie = jnp.where(l_thr == I16_MIN, I16_MAX, c2_ge - c2_gt)
    tie_rows = (n_tie > need_tie) & jnp.logical_not(all_visible)
    lane16 = lax.broadcasted_iota(I32, (tq, V7X_LANES), 1)

    def tie_cut():
        def body(it, jcut):
            cand = jcut + (jnp.int32(1) << (14 - it))
            cb = bcast16(cand)

            def pred(h, l, off):
                idx = (lane16 + off).astype(I16)
                tie_idx = jnp.where(h == hb, jnp.where(l == lb, idx, max16), max16)
                return jnp.where(tie_idx < cb, one16, zero16)
            return jnp.where(count(pred) <= need_tie, cand, jcut)
        return lax.fori_loop(0, 15, body, jnp.zeros((tq, 1), I32))

    no_cut = jnp.full((tq, 1), I16_MAX, I32)
    any_tie = jnp.max(jnp.where(tie_rows, 1, 0)) > 0
    jcut = lax.cond(any_tie, lambda: jnp.where(tie_rows, tie_cut(), no_cut), lambda: no_cut)
    l_thr = jnp.where(all_visible, I16_MAX, l_thr)
    jcut = jnp.where(all_visible, 0, jcut)

    rows = heads_per_kv * tq
    lhs = []
    for g in range(KVH_C):
        qh = head_rows(q_ref[0], g * heads_per_kv, heads_per_kv, HD_C) * (HD_C ** -0.5 * LOG2_E)
        lhs.append(qh.astype(BF16))
    hb_t = jnp.broadcast_to(h_thr.astype(I16), (tq, lt))
    lb_t = jnp.broadcast_to(l_thr.astype(I16), (tq, lt))
    lane_t16 = lane.astype(I16)
    v_lane = lax.broadcasted_iota(I32, (lt, KVH_C * HD_C), 1) // HD_C
    keep, drop = jnp.bfloat16(0.0), jnp.bfloat16(MASK_BIAS)

    def att_tile(j, carry):
        hi_t, lo_t = hi_ref[j], lo_ref[j]
        jc_t = jnp.broadcast_to(jnp.clip(jcut - j * lt, I16_MIN, I16_MAX).astype(I16), (tq, lt))
        tie = jnp.where(lo_t == lb_t, jnp.where(lane_t16 < jc_t, keep, drop), drop)
        in_bucket = jnp.where(lo_t > lb_t, keep, tie)
        bias = jnp.where(hi_t > hb_t, keep, jnp.where(hi_t == hb_t, in_bucket, drop))
        bias = bias.astype(F32)[None]
        v_t = v_ref[0, pl.ds(pl.multiple_of(j * lt, lt), lt), :]
        new = []
        for g in range(KVH_C):
            m_run, acc = carry[g]
            logits = jnp.dot(lhs[g], kt_ref[0, j, g * HD_C:(g + 1) * HD_C, :], preferred_element_type=F32)
            logits = (logits.reshape(heads_per_kv, tq, lt) + bias).reshape(rows, lt)
            m_new = jnp.maximum(m_run, jnp.max(logits, axis=1, keepdims=True))
            p = jnp.exp2(logits - m_new).astype(BF16)
            alpha = jnp.exp2(m_run - m_new)
            v_aug = jnp.where(v_lane == g, v_t, jnp.bfloat16(1.0))
            acc_new = alpha * acc + jnp.dot(p, v_aug, preferred_element_type=F32)
            new.append((m_new, acc_new))
        return tuple(new)

    init = tuple((jnp.full((rows, 1), -jnp.inf, F32), jnp.zeros((rows, KVH_C * HD_C), F32))
                 for _ in range(KVH_C))
    fin = lax.fori_loop(0, nt, att_tile, init)
    for g in range(KVH_C):
        acc = fin[g][1]
        den = acc[:, (1 - g) * HD_C:(1 - g) * HD_C + 1]
        og = acc / den
        for hh in range(heads_per_kv):
            h = g * heads_per_kv + hh
            o_ref[0, :, h * HD_C:(h + 1) * HD_C] = og[hh * tq:(hh + 1) * tq, g * HD_C:(g + 1) * HD_C]


PC_QI_BLOCK = 768 // 256
PC_MISC_BLOCK = 1024 // 128


def _dsa(pc, kt4, v_all, kit4, pos0, l_true):
    b, t, _ = pc.shape
    n_tiles, lt = kt4.shape[1], kt4.shape[3]
    lp = n_tiles * lt
    tq = _tile(t, 128)
    topk = min(TOPK_MAX, l_true // 4)
    assert lp <= I16_MAX and KVH_C == 2
    kern = functools.partial(_dsa_kernel, tq=tq, topk=topk, pos0=pos0, l_true=l_true, lt=lt)
    return pl.pallas_call(
        kern,
        grid=(b, t // tq),
        in_specs=[
            pl.BlockSpec((1, tq, W_C), lambda bi, i: (bi, i, 0)),
            pl.BlockSpec((1, tq, H_IDX * D_IDX), lambda bi, i: (bi, i, PC_QI_BLOCK)),
            pl.BlockSpec((1, tq, V7X_LANES), lambda bi, i: (bi, i, PC_MISC_BLOCK)),
            pl.BlockSpec((1, n_tiles, KVH_C * HD_C, lt), lambda bi, i: (bi, 0, 0, 0)),
            pl.BlockSpec((1, lp, KVH_C * HD_C), lambda bi, i: (bi, 0, 0)),
            pl.BlockSpec((1, n_tiles, D_IDX, lt), lambda bi, i: (bi, 0, 0, 0)),
        ],
        out_specs=pl.BlockSpec((1, tq, W_C), lambda bi, i: (bi, i, 0)),
        out_shape=jax.ShapeDtypeStruct((b, t, W_C), F32),
        scratch_shapes=[pltpu.VMEM((n_tiles, tq, lt), I16), pltpu.VMEM((n_tiles, tq, lt), I16)],
        compiler_params=_params(("arbitrary", "arbitrary")),
        name="dsa_attn",
    )(pc, pc, pc, kt4, v_all, kit4)


def _outproj_kernel(x_ref, mod_ref, oa_ref, hb_ref, oc_ref, w_ref, g_ref, b_ref, o_ref):
    bb, tm, d = x_ref.shape
    flat = lambda ref: ref[...].reshape(bb * tm, ref.shape[2])
    mix = (_dot(flat(oa_ref), w_ref[0:W_A, :]) + _dot(flat(hb_ref), w_ref[W_A:W_A + W_B, :])
           + _dot(flat(oc_ref), w_ref[W_A + W_B:, :]))
    y = ALPHA * x_ref[...] + (1.0 + mod_ref[:, 2:3, :]) * mix.reshape(bb, tm, d)
    o_ref[...] = _layer_norm(y, g_ref[...], b_ref[...])


def _outproj(x, mod, oa, hb, oc, w_out_bf, ln_g, ln_b):
    b, t, d = x.shape
    bb, tm = _row_blocks(b, t, 512)
    row = lambda bi, i: (bi, i, 0)
    mix_w = W_A + W_B + W_C
    return pl.pallas_call(
        _outproj_kernel,
        grid=(b // bb, t // tm),
        in_specs=[
            pl.BlockSpec((bb, tm, d), row),
            pl.BlockSpec((bb, 6, d), lambda bi, i: (bi, 0, 0)),
            pl.BlockSpec((bb, tm, W_A), row), pl.BlockSpec((bb, tm, W_B), row), pl.BlockSpec((bb, tm, W_C), row),
            pl.BlockSpec((mix_w, d), lambda bi, i: (0, 0)),
            pl.BlockSpec((1, d), lambda bi, i: (0, 0)), pl.BlockSpec((1, d), lambda bi, i: (0, 0)),
        ],
        out_specs=pl.BlockSpec((bb, tm, d), row),
        out_shape=jax.ShapeDtypeStruct((b, t, d), F32),
        compiler_params=_params(("arbitrary", "arbitrary")),
        name="out_proj",
    )(x, mod, oa, hb, oc, w_out_bf, ln_g, ln_b)


def _route_t(s_t, sb_t):
    per = N_EXPERTS // N_GROUPS
    gsc = []
    for gi in range(N_GROUPS):
        a, b, c, d = sb_t[gi * per:(gi + 1) * per]
        m1, n1 = jnp.maximum(a, b), jnp.minimum(a, b)
        m2, n2 = jnp.maximum(c, d), jnp.minimum(c, d)
        gsc.append(jnp.maximum(m1, m2) + jnp.maximum(jnp.minimum(m1, m2), jnp.maximum(n1, n2)))
    best, g_sel = gsc[0], jnp.zeros_like(gsc[0], dtype=I32)
    for gi in range(1, N_GROUPS):
        better = gsc[gi] > best
        best = jnp.where(better, gsc[gi], best)
        g_sel = jnp.where(better, gi, g_sel)
    cand = [jnp.where(g_sel == (e // per), sb_t[e], -jnp.inf) for e in range(N_EXPERTS)]
    v1, e1 = cand[0], jnp.zeros_like(g_sel)
    for e in range(1, N_EXPERTS):
        better = cand[e] > v1
        v1 = jnp.where(better, cand[e], v1)
        e1 = jnp.where(better, e, e1)
    v2, e2 = jnp.full_like(v1, -jnp.inf), jnp.full_like(e1, -1)
    for e in range(N_EXPERTS):
        better = (cand[e] > v2) & (e1 != e)
        v2 = jnp.where(better, cand[e], v2)
        e2 = jnp.where(better, e, e2)
    s1 = jnp.zeros_like(v1)
    s2 = jnp.zeros_like(v1)
    for e in range(N_EXPERTS):
        s1 = jnp.where(e1 == e, s_t[e], s1)
        s2 = jnp.where(e2 == e, s_t[e], s2)
    tot = s1 + s2
    return [jnp.where(e1 == e, s1 / tot, jnp.where(e2 == e, s2 / tot, 0.0)) for e in range(N_EXPERTS)]


def _moe_kernel(x_ref, mod_ref, wr_ref, br_ref, wg_ref, wu_ref, wd_ref, g_ref, b_ref, o_ref,
                u_ref, gate_ref, acc_ref):
    e = pl.program_id(2)
    bb, tm, d = x_ref.shape
    rows = bb * tm

    @pl.when(e == 0)
    def _():
        u = (x_ref[...] * (1.0 + mod_ref[:, 4:5, :]) + mod_ref[:, 3:4, :]).reshape(rows, d)
        u_ref[...] = u.astype(BF16)
        s = _sigmoid(_dot_exact(u, wr_ref[...]))
        sb = s + br_ref[...]
        s_t, sb_t = s.T, sb.T
        gates = _route_t([s_t[k:k + 1, :] for k in range(N_EXPERTS)],
                         [sb_t[k:k + 1, :] for k in range(N_EXPERTS)])
        row_id = lax.broadcasted_iota(I32, (N_EXPERTS, rows), 0)
        g16 = jnp.zeros((N_EXPERTS, rows), F32)
        for k in range(N_EXPERTS):
            g16 = jnp.where(row_id == k, gates[k], g16)
        g_t = jnp.concatenate([g16, jnp.zeros((V7X_LANES - N_EXPERTS, rows), F32)], axis=0)
        gate_ref[...] = g_t.T
        acc_ref[...] = jnp.zeros_like(acc_ref)

    u = u_ref[...]
    h = _silu(jnp.dot(u, wg_ref[0], preferred_element_type=F32)) * jnp.dot(u, wu_ref[0], preferred_element_type=F32)
    lane = lax.broadcasted_iota(I32, (rows, V7X_LANES), 1)
    gate = jnp.sum(jnp.where(lane == e, gate_ref[...], 0.0), axis=1, keepdims=True)
    acc_ref[...] += gate * jnp.dot(h.astype(BF16), wd_ref[0], preferred_element_type=F32)

    @pl.when(e == N_EXPERTS - 1)
    def _():
        y = ALPHA * x_ref[...] + (1.0 + mod_ref[:, 5:6, :]) * acc_ref[...].reshape(bb, tm, d)
        o_ref[...] = _layer_norm(y, g_ref[...], b_ref[...])


def _moe(x, mod, wr_p, br_p, wg_bf, wu_bf, wd_bf, ln_g, ln_b):
    b, t, d = x.shape
    bb, tm = _row_blocks(b, t, 1024)
    rows = bb * tm
    row = lambda bi, i, e: (bi, i, 0)
    const2 = lambda bi, i, e: (0, 0)
    return pl.pallas_call(
        _moe_kernel,
        grid=(b // bb, t // tm, N_EXPERTS),
        in_specs=[
            pl.BlockSpec((bb, tm, d), row),
            pl.BlockSpec((bb, 6, d), lambda bi, i, e: (bi, 0, 0)),
            pl.BlockSpec((d, V7X_LANES), const2), pl.BlockSpec((1, V7X_LANES), const2),
            pl.BlockSpec((1, d, D_FF_E), lambda bi, i, e: (e, 0, 0)),
            pl.BlockSpec((1, d, D_FF_E), lambda bi, i, e: (e, 0, 0)),
            pl.BlockSpec((1, D_FF_E, d), lambda bi, i, e: (e, 0, 0)),
            pl.BlockSpec((1, d), const2), pl.BlockSpec((1, d), const2),
        ],
        out_specs=pl.BlockSpec((bb, tm, d), row),
        out_shape=jax.ShapeDtypeStruct((b, t, d), F32),
        scratch_shapes=[pltpu.VMEM((rows, d), BF16), pltpu.VMEM((rows, V7X_LANES), F32), pltpu.VMEM((rows, d), F32)],
        compiler_params=_params(("arbitrary", "arbitrary", "arbitrary")),
        name="moe_ffn",
    )(x, mod, wr_p, br_p, wg_bf, wu_bf, wd_bf, ln_g, ln_b)


def _proj_columns():
    sizes = [H_A * DK_A, H_A * DK_A, W_A, W_A, GATE_RANK, 2 * W_B, W_B, W_B, H_B, H_B,
             W_C, KVH_C * HD_C, KVH_C * HD_C, H_IDX * D_IDX, D_IDX, H_IDX]
    offs = np.concatenate([[0], np.cumsum(sizes)])
    p = int(offs[-1])
    seg = lambda k: np.arange(offs[k], offs[k + 1])
    pad = lambda n: np.full((n,), p)
    qa, ka, va, ga, gr, qk, vb, ob, ib, fb, qc, kc, vc, qi, ki, wi = [seg(k) for k in range(16)]
    cols = np.concatenate([
        qa, ka, va, ga, gr, pad(128 - GATE_RANK),
        qk, vb, ob, ib, fb, pad(128 - 2 * H_B),
        qc, kc, vc, qi, ki, wi, pad(128 - D_IDX - H_IDX)])
    assert cols.shape[0] == P_TOTAL
    return cols, p


def _rope_tables(pos):
    def pattern(hd):
        half = hd // 2
        inv = jnp.power(ROPE_THETA, -jnp.arange(half, dtype=F32) / half)
        ang = pos.astype(F32)[:, None] * inv[None, :]
        cos = jnp.tile(jnp.concatenate([jnp.cos(ang), jnp.cos(ang)], -1), (1, V7X_LANES // hd))
        sin = jnp.tile(jnp.concatenate([-jnp.sin(ang), jnp.sin(ang)], -1), (1, V7X_LANES // hd))
        return cos, sin
    c64, s64 = pattern(HD_C)
    c32, s32 = pattern(D_IDX)
    keep = (jnp.arange(V7X_LANES) < D_IDX)[None, :]
    cm, sm = jnp.where(keep, c32, 1.0), jnp.where(keep, s32, 0.0)
    return (jnp.concatenate([c64, s64], -1), jnp.concatenate([c32, s32], -1), jnp.concatenate([cm, sm], -1))


def _pad_keys(a, lp):
    return jnp.pad(a, ((0, 0), (0, lp - a.shape[1]), (0, 0)))


def _layer(x, mod, lw, shared, tabs, pos0, past, first):
    b, t, d = x.shape
    consts = shared["consts"]
    l_true = t if past is None else t + past[0].shape[1]
    n_tiles = -(-l_true // KEY_TILE_MAX)
    lt = -(-l_true // (n_tiles * V7X_LANES)) * V7X_LANES
    lp = n_tiles * lt
    bb_proj, tm_proj = _row_blocks(b, t, 512)
    keys_from_proj = past is None and lp == t and bb_proj == 1 and lt % tm_proj == 0
    res = list(_inproj(x, mod, shared["ln_in_g"], shared["ln_in_b"], lw["w_p"], lw["bias_p"], lw["w_gate32"],
                       tabs, first, lt if keys_from_proj else None))
    if first:
        x = res.pop(0)
    pa, pb, pc = res[:3]

    if past is None:
        s0_t = jnp.zeros((b, H_A, DV_A, DK_A), F32)
    else:
        s0_t = jnp.swapaxes(past[3], -1, -2)
    oa, sfin_t = _gla(pa, s0_t, lw["wup_p"], lw["bup"], lw["gla_g"], consts)
    s_new = jnp.swapaxes(sfin_t, -1, -2)

    if past is None:
        conv0 = jnp.zeros((b, 8, 2 * W_B), F32)
        c0_st = jnp.zeros((b, W_B, MST_W), F32)
        m0 = jnp.zeros((b, 8, V7X_LANES), F32)
    else:
        conv0 = jnp.pad(past[7], ((0, 0), (8 - (CONV_W - 1), 0), (0, 0)))
        eye = jnp.eye(H_B, dtype=F32)
        c_bd = jnp.einsum('bhkv,hg->bhkgv', past[4], eye).reshape(b, W_B, W_B)
        n_bd = jnp.einsum('bhk,hg->bhkg', past[5], eye).reshape(b, W_B, H_B)
        c0_st = jnp.concatenate([c_bd, n_bd, jnp.zeros((b, W_B, V7X_LANES - H_B), F32)], -1)
        m0 = jnp.pad(jnp.broadcast_to(past[6][:, :, None], (b, H_B, V7X_LANES)), ((0, 0), (0, 8 - H_B), (0, 0)))
    hb, convo, cfin, mfin = _mlstm(pb, conv0, lw["cw_p"], lw["cb"], c0_st, m0, lw["mlstm_g"], consts)
    conv_new = convo[:, 8 - (CONV_W - 1):, :]
    cf = cfin[:, :, :W_B].reshape(b, H_B, DH_B, H_B, DH_B)
    c_new = jnp.stack([cf[:, h, :, h, :] for h in range(H_B)], axis=1)
    nf = cfin[:, :, W_B:W_B + H_B].reshape(b, H_B, DH_B, H_B)
    n_new = jnp.stack([nf[:, h, :, h] for h in range(H_B)], axis=1)
    m_new = mfin[:, :H_B, 0]

    k_new = pc[:, :, 512:640]
    v_new = pc[:, :, 640:768]
    ki_new = pc[:, :, 1024:1024 + D_IDX]
    if keys_from_proj:
        kt4, kit4, v_bf = res[3:]
    else:
        if past is None:
            k_all, v_all, ki_all = k_new, v_new, ki_new
        else:
            k_all = jnp.concatenate([past[0].reshape(b, -1, KVH_C * HD_C), k_new], axis=1)
            v_all = jnp.concatenate([past[1].reshape(b, -1, KVH_C * HD_C), v_new], axis=1)
            ki_all = jnp.concatenate([past[2], ki_new], axis=1)
        kt4 = _pad_keys(k_all, lp).astype(BF16).reshape(b, n_tiles, lt, -1).swapaxes(-1, -2)
        kit4 = _pad_keys(ki_all, lp).astype(BF16).reshape(b, n_tiles, lt, -1).swapaxes(-1, -2)
        v_bf = _pad_keys(v_all, lp).astype(BF16)
    oc = _dsa(pc, kt4, v_bf, kit4, pos0, l_true)

    x1 = _outproj(x, mod, oa, hb, oc, lw["w_out"], lw["ln1_g"], lw["ln1_b"])
    x2 = _moe(x1, mod, shared["wr_p"], shared["br_p"], lw["wg"], lw["wu"], lw["wd"], lw["ln2_g"], lw["ln2_b"])
    state = (k_new.reshape(b, t, KVH_C, HD_C), v_new.reshape(b, t, KVH_C, HD_C), ki_new,
             s_new, c_new, n_new, m_new, conv_new)
    return x2, state


def kernel(x_prompt, x_sample, c_prompt, c_sample, cache_k, cache_v, cache_kidx, state_gla, state_mlstm_C,
           state_mlstm_n, state_mlstm_m, state_mlstm_conv, ln_in_g, ln_in_b, w_ada, b_ada, w_in, b_in, gla_w_up,
           gla_b_up, gla_norm_g, mlstm_conv_w, mlstm_conv_b, mlstm_norm_g, w_out, ln1_g, ln1_b, w_router,
           b_router, w_gate, w_up, w_down, ln2_g, ln2_b):
    depth = w_in.shape[0]
    d = x_prompt.shape[-1]
    bp, tp, _ = x_prompt.shape
    bs, ts, _ = x_sample.shape
    past_len = cache_k.shape[2]

    cols, p = _proj_columns()
    tril = jnp.tril(jnp.ones((CHUNK, CHUNK), F32))
    ones_bd = (jnp.arange(W_A)[:, None] // DV_A == jnp.arange(W_A)[None, :] // DV_A).astype(BF16)
    shared = dict(
        consts=(tril, ones_bd),
        ln_in_g=ln_in_g.reshape(1, d), ln_in_b=ln_in_b.reshape(1, d),
        wr_p=jnp.pad(w_router, ((0, 0), (0, V7X_LANES - N_EXPERTS))),
        br_p=jnp.pad(b_router, (0, V7X_LANES - N_EXPERTS)).reshape(1, V7X_LANES),
    )
    layers = []
    for l in range(depth):
        w_ext = jnp.concatenate([w_in[l], jnp.zeros((d, 1), F32)], axis=1)
        b_ext = jnp.concatenate([b_in[l], jnp.zeros((1,), F32)])
        layers.append(dict(
            w_p=w_ext[:, cols].astype(BF16), bias_p=b_ext[cols].reshape(1, P_TOTAL),
            w_gate32=w_ext[:, cols[PA_W + PB_MISC:PA_W + PB_W]],
            wup_p=jnp.pad(gla_w_up[l], ((0, V7X_LANES - GATE_RANK), (0, 0))).astype(BF16),
            bup=gla_b_up[l].reshape(1, -1), gla_g=gla_norm_g[l].reshape(1, -1),
            cw_p=jnp.pad(mlstm_conv_w[l], ((0, 8 - CONV_W), (0, 0))), cb=mlstm_conv_b[l].reshape(1, -1),
            mlstm_g=mlstm_norm_g[l].reshape(1, -1),
            w_out=w_out[l].astype(BF16), ln1_g=ln1_g[l].reshape(1, d), ln1_b=ln1_b[l].reshape(1, d),
            wg=w_gate[l].astype(BF16), wu=w_up[l].astype(BF16), wd=w_down[l].astype(BF16),
            ln2_g=ln2_g[l].reshape(1, d), ln2_b=ln2_b[l].reshape(1, d),
        ))

    mod_all = _ada(jnp.concatenate([c_prompt, c_sample], axis=0), w_ada, b_ada)
    mod_all = mod_all.reshape(depth, bp + bs, 6, d)
    tabs_p = _rope_tables(jnp.arange(tp))
    tabs_s = _rope_tables(past_len + jnp.arange(ts))

    xp, xs = x_prompt, x_sample
    p_states, s_states = [], []
    for l in range(depth):
        xp, st_p = _layer(xp, mod_all[l, :bp], layers[l], shared, tabs_p, 0, None, l == 0)
        p_states.append(st_p)
        past = (cache_k[l], cache_v[l], cache_kidx[l], state_gla[l], state_mlstm_C[l], state_mlstm_n[l],
                state_mlstm_m[l], state_mlstm_conv[l])
        xs, st_s = _layer(xs, mod_all[l, bp:], layers[l], shared, tabs_s, past_len, past, l == 0)
        s_states.append(st_s)

    stk = lambda states, k: jnp.stack([st[k] for st in states], axis=0)
    return ((xp, xs) + tuple(stk(p_states, k) for k in range(8)) + tuple(stk(s_states, k) for k in range(8)))
```
